```python
import math
import jax, jax.numpy as jnp
from jax import lax
import numpy as np

D_MODEL = 2048
BATCH = 8
SEQ = 2048
DEPTH = 2
DEC_BATCH = 128
DEC_SEQ = 4
PAST_LEN = 8192
PAGE_SIZE = 128

N_MIXERS = 4
D_MIX = D_MODEL
D_GROUP = D_MIX // N_MIXERS
POOL_WINDOWS = (2, 4, 8, 16)
POOL_GROUPS = len(POOL_WINDOWS)
POOL_CH = D_GROUP // POOL_GROUPS
POOL_PAD = max(POOL_WINDOWS) - 1
CONV_WIDTH = 3
SWA_WINDOW = 128
SWA_BLOCK = SWA_WINDOW
SWA_HEAD_DIM = 64
SWA_Q_HEADS = D_GROUP // SWA_HEAD_DIM
SWA_KV_HEADS = 2
SWA_GROUP = SWA_Q_HEADS // SWA_KV_HEADS
SWA_KV_DIM = SWA_KV_HEADS * SWA_HEAD_DIM
SWA_SCALE = 1.0 / math.sqrt(SWA_HEAD_DIM)
MEM_TOKENS = 256
MEM_HEADS = 4
MEM_HEAD_DIM = D_GROUP // MEM_HEADS
MEM_SCALE = 1.0 / math.sqrt(MEM_HEAD_DIM)
D_FF = 4 * D_MODEL
RMS_EPS = 1e-6
SPLIT_OFFSETS = (D_GROUP, 2 * D_GROUP, 3 * D_GROUP, 4 * D_GROUP, 5 * D_GROUP,
                 5 * D_GROUP + SWA_KV_DIM, 5 * D_GROUP + 2 * SWA_KV_DIM)
D_IN = 6 * D_GROUP + 2 * SWA_KV_DIM

kernel_name = "hymba_pool_conv_swa_memory_decoder_step"


def rmsnorm(x, g):
    x32 = x.astype(jnp.float32)
    y = x32 * lax.rsqrt(jnp.mean(x32 * x32, axis=-1, keepdims=True) + RMS_EPS)
    return (y * g.astype(jnp.float32)).astype(x.dtype)


def multiscale_pool(u_ext, start_pos, t_new):
    b = u_ext.shape[0]
    cs = jnp.cumsum(u_ext.astype(jnp.float32), axis=1)
    cs = jnp.concatenate([jnp.zeros((b, 1, D_GROUP), jnp.float32), cs], axis=1)
    hi = cs[:, POOL_PAD + 1:]
    pos = start_pos + jnp.arange(t_new)
    outs = []
    for gi, w in enumerate(POOL_WINDOWS):
        sl = slice(gi * POOL_CH, (gi + 1) * POOL_CH)
        lo = cs[:, POOL_PAD + 1 - w:POOL_PAD + 1 - w + t_new, sl]
        cnt = jnp.minimum(pos + 1, w).astype(jnp.float32)
        outs.append((hi[:, :, sl] - lo) / cnt[None, :, None])
    return jnp.concatenate(outs, axis=-1).astype(u_ext.dtype)


def pool_mixer(u, u_prev, start_pos, w_pool, pool_scale):
    b, t, _ = u.shape
    u_ext = jnp.concatenate([u_prev, u], axis=1)
    d = (multiscale_pool(u_ext, start_pos, t) - u).reshape(b, t, POOL_GROUPS, POOL_CH)
    y = jnp.einsum('btgc,gcd->btgd', d, w_pool).reshape(b, t, D_GROUP)
    return y * pool_scale, u_ext[:, -POOL_PAD:]


def conv_mixer(hc, gate_b, gate_c, v_prev, conv_w):
    t = hc.shape[1]
    v_ext = jnp.concatenate([v_prev, gate_c * hc], axis=1)
    conv = conv_w[0] * v_ext[:, 0:t]
    for k in range(1, CONV_WIDTH):
        conv = conv + conv_w[k] * v_ext[:, k:k + t]
    return gate_b * conv, v_ext[:, -(CONV_WIDTH - 1):]


def sink_softmax(scores, mask, sink):
    s = jnp.where(mask, scores, -jnp.inf)
    m = jnp.maximum(jnp.max(s, axis=-1, keepdims=True), sink)
    e = jnp.exp(s - m)
    return e / (jnp.sum(e, axis=-1, keepdims=True) + jnp.exp(sink - m))


def swa_banded(q, k, v, sinks):
    b, t = q.shape[0], q.shape[1]
    nb = t // SWA_BLOCK
    qb = q.reshape(b, nb, SWA_BLOCK, SWA_KV_HEADS, SWA_GROUP, SWA_HEAD_DIM)
    def band(a):
        ap = jnp.concatenate([jnp.zeros_like(a[:, :SWA_BLOCK]), a], axis=1)
        ap = ap.reshape(b, nb + 1, SWA_BLOCK, SWA_KV_HEADS, SWA_HEAD_DIM)
        return jnp.concatenate([ap[:, :-1], ap[:, 1:]], axis=2)
    kb, vb = band(k), band(v)
    i = jnp.arange(SWA_BLOCK)[None, :, None]
    j = jnp.arange(2 * SWA_BLOCK)[None, None, :]
    n = jnp.arange(nb)[:, None, None]
    diff = i + SWA_BLOCK - j
    kpos = (n - 1) * SWA_BLOCK + j
    mask = (diff >= 0) & (diff < SWA_WINDOW) & (kpos >= 0)
    scores = jnp.einsum('bnqhgd,bnkhd->bnhgqk', qb, kb,
                        preferred_element_type=jnp.float32) * SWA_SCALE
    sink = sinks.astype(jnp.float32).reshape(SWA_KV_HEADS, SWA_GROUP)[None, None, :, :, None, None]
    p = sink_softmax(scores, mask[None, :, None, None], sink).astype(v.dtype)
    o = jnp.einsum('bnhgqk,bnkhd->bnqhgd', p, vb)
    return o.reshape(b, t, D_GROUP), k[:, -SWA_WINDOW:], v[:, -SWA_WINDOW:]


def swa_buffered(q, k, v, k_prev, v_prev, sinks):
    b, t = q.shape[0], q.shape[1]
    k_all = jnp.concatenate([k_prev, k], axis=1)
    v_all = jnp.concatenate([v_prev, v], axis=1)
    i = jnp.arange(t)[:, None]
    j = jnp.arange(SWA_WINDOW + t)[None, :]
    diff = i + SWA_WINDOW - j
    mask = (diff >= 0) & (diff < SWA_WINDOW)
    scores = jnp.einsum('bqhgd,bkhd->bhgqk', q, k_all,
                        preferred_element_type=jnp.float32) * SWA_SCALE
    sink = sinks.astype(jnp.float32).reshape(SWA_KV_HEADS, SWA_GROUP)[None, :, :, None, None]
    p = sink_softmax(scores, mask[None, None, None], sink).astype(v.dtype)
    o = jnp.einsum('bhgqk,bkhd->bqhgd', p, v_all)
    return o.reshape(b, t, D_GROUP), k_all[:, t:], v_all[:, t:]


def memory_kv(mem, g_mem, w_mem_kv):
    b, m, _ = mem.shape
    kv = rmsnorm(mem, g_mem) @ w_mem_kv
    k, v = jnp.split(kv, 2, axis=-1)
    return (k.reshape(b, m, MEM_HEADS, MEM_HEAD_DIM), v.reshape(b, m, MEM_HEADS, MEM_HEAD_DIM))


def memory_attention(q, mem_k, mem_v):
    b, t = q.shape[0], q.shape[1]
    s = jnp.einsum('bqhd,bkhd->bhqk', q, mem_k, preferred_element_type=jnp.float32) * MEM_SCALE
    p = jax.nn.softmax(s, axis=-1).astype(mem_v.dtype)
    return jnp.einsum('bhqk,bkhd->bqhd', p, mem_v).reshape(b, t, D_GROUP)


def trunk_layer(x, start_pos, pool_prev, conv_prev, swa_k_prev, swa_v_prev, mem_k, mem_v,
                g_mix_pre, w_in, w_pool, pool_scale, conv_w, swa_sinks, w_out, g_mix_post,
                g_mlp_pre, w_up, w_down, g_mlp_post):
    b, t, _ = x.shape
    h = rmsnorm(x, g_mix_pre)
    proj = h @ w_in
    u, hc, gb, gc, q, k, v, qm = jnp.split(proj, SPLIT_OFFSETS, axis=-1)
    y_pool, new_pool = pool_mixer(u, pool_prev, start_pos, w_pool, pool_scale)
    y_conv, new_conv = conv_mixer(hc, gb, gc, conv_prev, conv_w)
    q = q.reshape(b, t, SWA_KV_HEADS, SWA_GROUP, SWA_HEAD_DIM)
    k = k.reshape(b, t, SWA_KV_HEADS, SWA_HEAD_DIM)
    v = v.reshape(b, t, SWA_KV_HEADS, SWA_HEAD_DIM)
    if swa_k_prev is None:
        y_swa, new_k, new_v = swa_banded(q, k, v, swa_sinks)
    else:
        y_swa, new_k, new_v = swa_buffered(q, k, v, swa_k_prev, swa_v_prev, swa_sinks)
    y_mem = memory_attention(qm.reshape(b, t, MEM_HEADS, MEM_HEAD_DIM), mem_k, mem_v)
    mix = jnp.concatenate([y_pool, y_conv, y_swa, y_mem], axis=-1) @ w_out
    x = x + rmsnorm(mix, g_mix_post)
    h2 = rmsnorm(x, g_mlp_pre)
    ff = jnp.square(jax.nn.relu(h2 @ w_up)) @ w_down
    x = x + rmsnorm(ff, g_mlp_post)
    return x, new_pool, new_conv, new_k, new_v


def setup_inputs(seed: int = 0) -> dict:
    key = jax.random.key(seed)
    ks = jax.random.split(key, 24)
    f32 = jnp.float32
    def nrm(k, shape, scale=1.0):
        return jax.random.normal(k, shape, f32) * scale
    def gain(k, shape):
        return 1.0 + 0.05 * jax.random.normal(k, shape, f32)
    return {
        "x_prompt": nrm(ks[0], (BATCH, SEQ, D_MODEL)),
        "x_sample": nrm(ks[1], (DEC_BATCH, DEC_SEQ, D_MODEL)),
        "mem_prompt": nrm(ks[2], (BATCH, MEM_TOKENS, D_MODEL)),
        "state_pool": nrm(ks[3], (DEPTH, DEC_BATCH, POOL_PAD, D_GROUP)),
        "state_conv": nrm(ks[4], (DEPTH, DEC_BATCH, CONV_WIDTH - 1, D_GROUP)),
        "cache_swa_k": nrm(ks[5], (DEPTH, DEC_BATCH, SWA_WINDOW, SWA_KV_HEADS, SWA_HEAD_DIM)),
        "cache_swa_v": nrm(ks[6], (DEPTH, DEC_BATCH, SWA_WINDOW, SWA_KV_HEADS, SWA_HEAD_DIM)),
        "cache_mem_k": nrm(ks[7], (DEPTH, DEC_BATCH, MEM_TOKENS, MEM_HEADS, MEM_HEAD_DIM)),
        "cache_mem_v": nrm(ks[8], (DEPTH, DEC_BATCH, MEM_TOKENS, MEM_HEADS, MEM_HEAD_DIM)),
        "g_mix_pre": gain(ks[9], (DEPTH, D_MODEL)),
        "w_in": nrm(ks[10], (DEPTH, D_MODEL, D_IN), D_MODEL ** -0.5),
        "w_pool": nrm(ks[11], (DEPTH, POOL_GROUPS, POOL_CH, POOL_CH), POOL_CH ** -0.5),
        "pool_scale": gain(ks[12], (DEPTH, D_GROUP)),
        "conv_w": nrm(ks[13], (DEPTH, CONV_WIDTH, D_GROUP), CONV_WIDTH ** -0.5),
        "swa_sinks": nrm(ks[14], (DEPTH, SWA_Q_HEADS), 0.5),
        "g_mem": gain(ks[15], (DEPTH, D_MODEL)),
        "w_mem_kv": nrm(ks[16], (DEPTH, D_MODEL, 2 * D_GROUP), D_MODEL ** -0.5),
        "w_out": nrm(ks[17], (DEPTH, D_MIX, D_MODEL), D_MIX ** -0.5),
        "g_mix_post": gain(ks[18], (DEPTH, D_MODEL)),
        "g_mlp_pre": gain(ks[19], (DEPTH, D_MODEL)),
        "w_up": nrm(ks[20], (DEPTH, D_MODEL, D_FF), D_MODEL ** -0.5),
        "w_down": nrm(ks[21], (DEPTH, D_FF, D_MODEL), D_FF ** -0.5),
        "g_mlp_post": gain(ks[22], (DEPTH, D_MODEL)),
    }


def reference(x_prompt, x_sample, mem_prompt, state_pool, state_conv, cache_swa_k, cache_swa_v,
              cache_mem_k, cache_mem_v, g_mix_pre, w_in, w_pool, pool_scale, conv_w, swa_sinks,
              g_mem, w_mem_kv, w_out, g_mix_post, g_mlp_pre, w_up, w_down, g_mlp_post):
    yp, ys = x_prompt, x_sample
    bp = x_prompt.shape[0]
    pool_p, pool_s, conv_p, conv_s = [], [], [], []
    kp_l, ks_l, vp_l, vs_l, mk_l, mv_l = [], [], [], [], [], []
    for l in range(DEPTH):
        weights = (g_mix_pre[l], w_in[l], w_pool[l], pool_scale[l], conv_w[l], swa_sinks[l],
                   w_out[l], g_mix_post[l], g_mlp_pre[l], w_up[l], w_down[l], g_mlp_post[l])
        mk, mv = memory_kv(mem_prompt, g_mem[l], w_mem_kv[l])
        yp, pp, cp, kp, vp = trunk_layer(
            yp, 0,
            jnp.zeros((bp, POOL_PAD, D_GROUP), x_prompt.dtype),
            jnp.zeros((bp, CONV_WIDTH - 1, D_GROUP), x_prompt.dtype),
            None, None, mk, mv, *weights)
        ys, ps, cs, kss, vss = trunk_layer(
            ys, PAST_LEN, state_pool[l], state_conv[l], cache_swa_k[l], cache_swa_v[l],
            cache_mem_k[l], cache_mem_v[l], *weights)
        pool_p.append(pp); pool_s.append(ps); conv_p.append(cp); conv_s.append(cs)
        kp_l.append(kp); ks_l.append(kss); vp_l.append(vp); vs_l.append(vss)
        mk_l.append(mk); mv_l.append(mv)
    return (yp, ys,
            jnp.stack(pool_p), jnp.stack(pool_s),
            jnp.stack(conv_p), jnp.stack(conv_s),
            jnp.stack(kp_l), jnp.stack(ks_l),
            jnp.stack(vp_l), jnp.stack(vs_l),
            jnp.stack(mk_l), jnp.stack(mv_l))
```

```python
import functools
import math

import jax
import jax.numpy as jnp
from jax import lax
from jax.experimental import pallas as pl
from jax.experimental.pallas import tpu as pltpu

F32 = jnp.float32
BF16 = jnp.bfloat16

LANES = 128
D_GROUP = 512
POOL_WINDOWS = (2, 4, 8, 16)
POOL_CH = D_GROUP // len(POOL_WINDOWS)
POOL_PAD = max(POOL_WINDOWS) - 1
POOL_HALO = 16
CONV_WIDTH = 3
CONV_HALO = 8
SWA_WINDOW = 128
SWA_HEAD_DIM = 64
SWA_KV_DIM = 128
SWA_SCALE = 1.0 / math.sqrt(SWA_HEAD_DIM)
MEM_HEADS = 4
MEM_HEAD_DIM = D_GROUP // MEM_HEADS
MEM_SCALE = 1.0 / math.sqrt(MEM_HEAD_DIM)
RMS_EPS = 1e-6
PAST_LEN = 8192
OFF_U, OFF_HC, OFF_GB, OFF_GC, OFF_Q = 0, 512, 1024, 1536, 2048
OFF_K, OFF_V, OFF_QM, D_IN = 2560, 2688, 2816, 3328
MIX_POOL, MIX_CONV, MIX_SWA, MIX_MEM = 0, 512, 1024, 1536

VMEM_LIMIT_BYTES = 56 * 1024 * 1024


def _rmsnorm(x, g):
    ms = jnp.mean(x * x, axis=-1, keepdims=True)
    return x * lax.rsqrt(ms + RMS_EPS) * g


def _params(*sem):
    return pltpu.CompilerParams(dimension_semantics=sem, vmem_limit_bytes=VMEM_LIMIT_BYTES)


def _resident(shape):
    zeros = (0,) * len(shape)
    return pl.BlockSpec(shape, lambda *_: zeros, pipeline_mode=pl.Buffered(1))


def _norm_matmul_kernel(x_ref, g_ref, w_ref, o_ref):
    h = _rmsnorm(x_ref[...], g_ref[...]).astype(BF16)
    o_ref[...] = jnp.dot(h, w_ref[...], preferred_element_type=F32)


def _norm_matmul(x, g, w, tm):
    m, k = x.shape
    n = w.shape[1]
    return pl.pallas_call(
        _norm_matmul_kernel,
        grid=(m // tm,),
        in_specs=[pl.BlockSpec((tm, k), lambda i: (i, 0)), _resident((1, k)), _resident((k, n))],
        out_specs=pl.BlockSpec((tm, n), lambda i: (i, 0)),
        out_shape=jax.ShapeDtypeStruct((m, n), F32),
        compiler_params=_params("arbitrary"),
        name="norm_matmul",
    )(x, g, w)


def _matmul_norm_res_kernel(a_ref, w_ref, g_ref, x_ref, o_ref):
    y = jnp.dot(a_ref[...].astype(BF16), w_ref[...], preferred_element_type=F32)
    o_ref[...] = x_ref[...] + _rmsnorm(y, g_ref[...])


def _matmul_norm_res(a, w, g, x, tm):
    m, k = a.shape
    n = w.shape[1]
    return pl.pallas_call(
        _matmul_norm_res_kernel,
        grid=(m // tm,),
        in_specs=[pl.BlockSpec((tm, k), lambda i: (i, 0)), _resident((k, n)), _resident((1, n)),
                  pl.BlockSpec((tm, n), lambda i: (i, 0))],
        out_specs=pl.BlockSpec((tm, n), lambda i: (i, 0)),
        out_shape=jax.ShapeDtypeStruct((m, n), F32),
        compiler_params=_params("arbitrary"),
        name="matmul_norm_res",
    )(a, w, g, x)


def _mlp_kernel(x_ref, gpre_ref, wup_ref, wdown_ref, gpost_ref, o_ref, h_ref):
    f = pl.program_id(1)

    @pl.when(f == 0)
    def _():
        h_ref[...] = _rmsnorm(x_ref[...], gpre_ref[...]).astype(BF16)
        o_ref[...] = jnp.zeros(o_ref.shape, F32)

    a = jnp.dot(h_ref[...], wup_ref[...], preferred_element_type=F32)
    a = jnp.square(jnp.maximum(a, 0.0)).astype(BF16)
    o_ref[...] += jnp.dot(a, wdown_ref[...], preferred_element_type=F32)

    @pl.when(f == pl.num_programs(1) - 1)
    def _():
        o_ref[...] = x_ref[...] + _rmsnorm(o_ref[...], gpost_ref[...])


def _mlp(x, g_pre, w_up, w_down, g_post, tm, tf):
    m, d = x.shape
    ff = w_up.shape[1]
    return pl.pallas_call(
        _mlp_kernel,
        grid=(m // tm, ff // tf),
        in_specs=[pl.BlockSpec((tm, d), lambda i, f: (i, 0)), _resident((1, d)),
                  pl.BlockSpec((d, tf), lambda i, f: (0, f)),
                  pl.BlockSpec((tf, d), lambda i, f: (f, 0)), _resident((1, d))],
        out_specs=pl.BlockSpec((tm, d), lambda i, f: (i, 0)),
        out_shape=jax.ShapeDtypeStruct((m, d), F32),
        scratch_shapes=[pltpu.VMEM((tm, d), BF16)],
        compiler_params=_params("arbitrary", "arbitrary"),
        name="mlp",
    )(x, g_pre, w_up, w_down, g_post)


def _pool_mixer(uext_ref, u, pos0, wpool_ref, pscale_ref):
    rows = u.shape[0]
    uext_ref[POOL_HALO:POOL_HALO + rows, :] = u
    pos = pos0 + lax.broadcasted_iota(jnp.int32, (rows, POOL_CH), 0)
    outs = []
    for gi, w in enumerate(POOL_WINDOWS):
        c0 = gi * POOL_CH
        s = u[:, c0:c0 + POOL_CH]
        for back in range(1, w):
            s = s + uext_ref[POOL_HALO - back:POOL_HALO - back + rows, c0:c0 + POOL_CH]
        cnt = jnp.minimum(pos + 1, w).astype(F32)
        d = (s / cnt - u[:, c0:c0 + POOL_CH]).astype(BF16)
        y = jnp.dot(d, wpool_ref[gi], preferred_element_type=F32)
        outs.append(y * pscale_ref[:, c0:c0 + POOL_CH])
    return outs


def _conv_mixer(cext_ref, hc, gb, gc, convw_ref):
    rows = hc.shape[0]
    cext_ref[CONV_HALO:CONV_HALO + rows, :] = gc * hc
    conv = convw_ref[0:1, :] * cext_ref[CONV_HALO - 2:CONV_HALO - 2 + rows, :]
    for kk in range(1, CONV_WIDTH):
        lo = CONV_HALO - 2 + kk
        conv = conv + convw_ref[kk:kk + 1, :] * cext_ref[lo:lo + rows, :]
    return gb * conv


def _swa_mixer(q, kwin, vwin, sinks_ref, first_key):
    rows = q.shape[0]
    nk = kwin.shape[0]
    low = lax.broadcasted_iota(jnp.int32, (nk, LANES), 1) < SWA_HEAD_DIM
    kroll = pltpu.roll(kwin, SWA_HEAD_DIM, 1)
    vroll = pltpu.roll(vwin, SWA_HEAD_DIM, 1)
    ind_a = jnp.where(low, 1.0, 0.0)
    ind_b = 1.0 - ind_a
    qi = lax.broadcasted_iota(jnp.int32, (rows, 2 * nk), 0)
    kj = lax.broadcasted_iota(jnp.int32, (rows, 2 * nk), 1) & (nk - 1)
    valid = (kj > qi) & (kj <= qi + SWA_WINDOW) & (kj >= first_key)
    bias = jnp.where(valid, 0.0, -jnp.inf)
    low_q = lax.broadcasted_iota(jnp.int32, (rows, LANES), 1) < SWA_HEAD_DIM
    outs = []
    for h in range(2):
        if h == 0:
            k_a, k_b = jnp.where(low, kwin, 0.0), jnp.where(low, 0.0, kroll)
            v_a, v_b = jnp.where(low, vwin, 0.0), jnp.where(low, 0.0, vroll)
        else:
            k_a, k_b = jnp.where(low, kroll, 0.0), jnp.where(low, 0.0, kwin)
            v_a, v_b = jnp.where(low, vroll, 0.0), jnp.where(low, 0.0, vwin)
        kst = jnp.concatenate([k_a, k_b], axis=0).astype(BF16)
        vst = jnp.concatenate([jnp.concatenate([v_a, ind_a], axis=1),
                               jnp.concatenate([v_b, ind_b], axis=1)], axis=0).astype(BF16)
        for r in range(2):
            pair = 2 * h + r
            qp = (q[:, pair * LANES:(pair + 1) * LANES] * SWA_SCALE).astype(BF16)
            s = lax.dot_general(qp, kst, (((1,), (1,)), ((), ())),
                                preferred_element_type=F32) + bias
            sink_a = sinks_ref[4 * h + 2 * r]
            sink_b = sinks_ref[4 * h + 2 * r + 1]
            m_a = jnp.maximum(jnp.max(s[:, :nk], axis=1, keepdims=True), sink_a)
            m_b = jnp.maximum(jnp.max(s[:, nk:], axis=1, keepdims=True), sink_b)
            p = jnp.concatenate([jnp.exp(s[:, :nk] - m_a), jnp.exp(s[:, nk:] - m_b)],
                                axis=1).astype(BF16)
            o = jnp.dot(p, vst, preferred_element_type=F32)
            den = o[:, LANES:] + jnp.where(low_q, jnp.exp(sink_a - m_a), jnp.exp(sink_b - m_b))
            outs.append(o[:, :LANES] / den)
    return outs


def _mem_mixer(qm, mk, mv):
    ones = jnp.ones((mk.shape[0], LANES), BF16)
    outs = []
    for hm in range(MEM_HEADS):
        c0 = hm * MEM_HEAD_DIM
        s = lax.dot_general(qm[:, c0:c0 + MEM_HEAD_DIM].astype(BF16), mk[:, c0:c0 + MEM_HEAD_DIM],
                            (((1,), (1,)), ((), ())), preferred_element_type=F32) * MEM_SCALE
        m = jnp.max(s, axis=1, keepdims=True)
        p = jnp.exp(s - m).astype(BF16)
        vext = jnp.concatenate([mv[:, c0:c0 + MEM_HEAD_DIM], ones], axis=1)
        o = jnp.dot(p, vext, preferred_element_type=F32)
        outs.append(o[:, :MEM_HEAD_DIM] / o[:, MEM_HEAD_DIM:])
    return outs


def _prompt_mixer_kernel(sinks_ref, proj_ref, mk_ref, mv_ref, wpool_ref, pscale_ref, convw_ref,
                         mix_ref, npool_ref, nconv_ref, nk_ref, nv_ref,
                         uext_ref, cext_ref, kext_ref, vext_ref, *, tq):
    j = pl.program_id(1)

    @pl.when(j == 0)
    def _():
        uext_ref[0:POOL_HALO, :] = jnp.zeros((POOL_HALO, D_GROUP), F32)
        cext_ref[0:CONV_HALO, :] = jnp.zeros((CONV_HALO, D_GROUP), F32)
        kext_ref[0:SWA_WINDOW, :] = jnp.zeros((SWA_WINDOW, SWA_KV_DIM), F32)
        vext_ref[0:SWA_WINDOW, :] = jnp.zeros((SWA_WINDOW, SWA_KV_DIM), F32)

    u = proj_ref[:, OFF_U:OFF_U + D_GROUP]
    for gi, y in enumerate(_pool_mixer(uext_ref, u, j * tq, wpool_ref, pscale_ref)):
        mix_ref[:, MIX_POOL + gi * POOL_CH:MIX_POOL + (gi + 1) * POOL_CH] = y.astype(mix_ref.dtype)
    npool_ref[...] = uext_ref[tq + POOL_HALO - POOL_PAD:tq + POOL_HALO, :]
    uext_ref[0:POOL_HALO, :] = uext_ref[tq:tq + POOL_HALO, :]

    y = _conv_mixer(cext_ref, proj_ref[:, OFF_HC:OFF_HC + D_GROUP], proj_ref[:, OFF_GB:OFF_GB + D_GROUP],
                    proj_ref[:, OFF_GC:OFF_GC + D_GROUP], convw_ref)
    mix_ref[:, MIX_CONV:MIX_CONV + D_GROUP] = y.astype(mix_ref.dtype)
    nconv_ref[...] = cext_ref[tq + CONV_HALO - 2:tq + CONV_HALO, :]
    cext_ref[0:CONV_HALO, :] = cext_ref[tq:tq + CONV_HALO, :]

    kext_ref[SWA_WINDOW:SWA_WINDOW + tq, :] = proj_ref[:, OFF_K:OFF_K + SWA_KV_DIM]
    vext_ref[SWA_WINDOW:SWA_WINDOW + tq, :] = proj_ref[:, OFF_V:OFF_V + SWA_KV_DIM]
    for sb in range(tq // SWA_WINDOW):
        r0 = sb * SWA_WINDOW
        first_key = jnp.maximum(SWA_WINDOW - (j * tq + r0), 0)
        outs = _swa_mixer(proj_ref[r0:r0 + SWA_WINDOW, OFF_Q:OFF_Q + D_GROUP],
                          kext_ref[r0:r0 + 2 * SWA_WINDOW, :], vext_ref[r0:r0 + 2 * SWA_WINDOW, :],
                          sinks_ref, first_key)
        for pair, y in enumerate(outs):
            c0 = MIX_SWA + pair * LANES
            mix_ref[r0:r0 + SWA_WINDOW, c0:c0 + LANES] = y.astype(mix_ref.dtype)
    nk_ref[...] = kext_ref[tq:tq + SWA_WINDOW, :]
    nv_ref[...] = vext_ref[tq:tq + SWA_WINDOW, :]
    kext_ref[0:SWA_WINDOW, :] = kext_ref[tq:tq + SWA_WINDOW, :]
    vext_ref[0:SWA_WINDOW, :] = vext_ref[tq:tq + SWA_WINDOW, :]

    outs = _mem_mixer(proj_ref[:, OFF_QM:OFF_QM + D_GROUP], mk_ref[...].astype(BF16), mv_ref[...].astype(BF16))
    for hm, y in enumerate(outs):
        c0 = MIX_MEM + hm * MEM_HEAD_DIM
        mix_ref[:, c0:c0 + MEM_HEAD_DIM] = y.astype(mix_ref.dtype)


def _prompt_mixer(sinks, proj, mk, mv, w_pool, pool_scale, conv_w, tq):
    b, t, _ = proj.shape
    nmem = mk.shape[1]
    per_seq = lambda i, j: (i, 0, 0)
    return pl.pallas_call(
        functools.partial(_prompt_mixer_kernel, tq=tq),
        grid=(b, t // tq),
        in_specs=[pl.BlockSpec(memory_space=pltpu.SMEM),
                  pl.BlockSpec((None, tq, D_IN), lambda i, j: (i, j, 0)),
                  pl.BlockSpec((None, nmem, D_GROUP), per_seq),
                  pl.BlockSpec((None, nmem, D_GROUP), per_seq),
                  _resident(w_pool.shape), _resident(pool_scale.shape), _resident(conv_w.shape)],
        out_specs=[pl.BlockSpec((None, tq, 4 * D_GROUP), lambda i, j: (i, j, 0)),
                   pl.BlockSpec((None, POOL_PAD, D_GROUP), per_seq),
                   pl.BlockSpec((None, CONV_WIDTH - 1, D_GROUP), per_seq),
                   pl.BlockSpec((None, SWA_WINDOW, SWA_KV_DIM), per_seq),
                   pl.BlockSpec((None, SWA_WINDOW, SWA_KV_DIM), per_seq)],
        out_shape=[jax.ShapeDtypeStruct((b, t, 4 * D_GROUP), BF16),
                   jax.ShapeDtypeStruct((b, POOL_PAD, D_GROUP), F32),
                   jax.ShapeDtypeStruct((b, CONV_WIDTH - 1, D_GROUP), F32),
                   jax.ShapeDtypeStruct((b, SWA_WINDOW, SWA_KV_DIM), F32),
                   jax.ShapeDtypeStruct((b, SWA_WINDOW, SWA_KV_DIM), F32)],
        scratch_shapes=[pltpu.VMEM((POOL_HALO + tq, D_GROUP), F32),
                        pltpu.VMEM((CONV_HALO + tq, D_GROUP), F32),
                        pltpu.VMEM((SWA_WINDOW + tq, SWA_KV_DIM), F32),
                        pltpu.VMEM((SWA_WINDOW + tq, SWA_KV_DIM), F32)],
        compiler_params=_params("arbitrary", "arbitrary"),
        name="prompt_mixer",
    )(sinks, proj, mk, mv, w_pool, pool_scale, conv_w)


def _sample_mixer_kernel(sinks_ref, proj_ref, pool_ref, conv_ref, kc_ref, vc_ref, mk_ref, mv_ref,
                         wpool_ref, pscale_ref, convw_ref,
                         mix_ref, npool_ref, nconv_ref, nk_ref, nv_ref,
                         uext_ref, cext_ref, kext_ref, vext_ref, *, nb, t_new, pos0):
    uext_ref[...] = jnp.zeros(uext_ref.shape, F32)
    cext_ref[...] = jnp.zeros(cext_ref.shape, F32)
    kext_ref[...] = jnp.zeros(kext_ref.shape, F32)
    vext_ref[...] = jnp.zeros(vext_ref.shape, F32)

    def body(b, carry):
        uext_ref[POOL_HALO - POOL_PAD:POOL_HALO, :] = pool_ref[b]
        u = proj_ref[b, :, OFF_U:OFF_U + D_GROUP]
        for gi, y in enumerate(_pool_mixer(uext_ref, u, pos0, wpool_ref, pscale_ref)):
            mix_ref[b, :, MIX_POOL + gi * POOL_CH:MIX_POOL + (gi + 1) * POOL_CH] = y
        npool_ref[b] = uext_ref[t_new + POOL_HALO - POOL_PAD:t_new + POOL_HALO, :]

        cext_ref[CONV_HALO - 2:CONV_HALO, :] = conv_ref[b]
        y = _conv_mixer(cext_ref, proj_ref[b, :, OFF_HC:OFF_HC + D_GROUP],
                        proj_ref[b, :, OFF_GB:OFF_GB + D_GROUP],
                        proj_ref[b, :, OFF_GC:OFF_GC + D_GROUP], convw_ref)
        mix_ref[b, :, MIX_CONV:MIX_CONV + D_GROUP] = y
        nconv_ref[b] = cext_ref[t_new + CONV_HALO - 2:t_new + CONV_HALO, :]

        kext_ref[0:SWA_WINDOW, :] = kc_ref[b]
        vext_ref[0:SWA_WINDOW, :] = vc_ref[b]
        kext_ref[SWA_WINDOW:SWA_WINDOW + t_new, :] = proj_ref[b, :, OFF_K:OFF_K + SWA_KV_DIM]
        vext_ref[SWA_WINDOW:SWA_WINDOW + t_new, :] = proj_ref[b, :, OFF_V:OFF_V + SWA_KV_DIM]
        outs = _swa_mixer(proj_ref[b, :, OFF_Q:OFF_Q + D_GROUP], kext_ref[...], vext_ref[...],
                          sinks_ref, max(SWA_WINDOW - pos0, 0))
        for pair, y in enumerate(outs):
            c0 = MIX_SWA + pair * LANES
            mix_ref[b, :, c0:c0 + LANES] = y
        nk_ref[b] = kext_ref[t_new:t_new + SWA_WINDOW, :]
        nv_ref[b] = vext_ref[t_new:t_new + SWA_WINDOW, :]

        outs = _mem_mixer(proj_ref[b, :, OFF_QM:OFF_QM + D_GROUP], mk_ref[b].astype(BF16),
                          mv_ref[b].astype(BF16))
        for hm, y in enumerate(outs):
            c0 = MIX_MEM + hm * MEM_HEAD_DIM
            mix_ref[b, :, c0:c0 + MEM_HEAD_DIM] = y
        return carry

    lax.fori_loop(0, nb, body, 0)


def _sample_mixer(sinks, proj, pool, conv, kc, vc, mk, mv, w_pool, pool_scale, conv_w, nb, pos0):
    b, t_new, _ = proj.shape
    nmem = mk.shape[1]
    grp = lambda i: (i, 0, 0)
    blk = lambda a: pl.BlockSpec((nb,) + a.shape[1:], grp)
    return pl.pallas_call(
        functools.partial(_sample_mixer_kernel, nb=nb, t_new=t_new, pos0=pos0),
        grid=(b // nb,),
        in_specs=[pl.BlockSpec(memory_space=pltpu.SMEM), blk(proj), blk(pool), blk(conv), blk(kc), blk(vc),
                  blk(mk), blk(mv),
                  _resident(w_pool.shape), _resident(pool_scale.shape), _resident(conv_w.shape)],
        out_specs=[pl.BlockSpec((nb, t_new, 4 * D_GROUP), grp), blk(pool), blk(conv), blk(kc), blk(vc)],
        out_shape=[jax.ShapeDtypeStruct((b, t_new, 4 * D_GROUP), F32),
                   jax.ShapeDtypeStruct(pool.shape, F32), jax.ShapeDtypeStruct(conv.shape, F32),
                   jax.ShapeDtypeStruct(kc.shape, F32), jax.ShapeDtypeStruct(vc.shape, F32)],
        scratch_shapes=[pltpu.VMEM((POOL_HALO + 8, D_GROUP), F32),
                        pltpu.VMEM((CONV_HALO + 8, D_GROUP), F32),
                        pltpu.VMEM((2 * SWA_WINDOW, SWA_KV_DIM), F32),
                        pltpu.VMEM((2 * SWA_WINDOW, SWA_KV_DIM), F32)],
        compiler_params=_params("arbitrary"),
        name="sample_mixer",
    )(sinks, proj, pool, conv, kc, vc, mk, mv, w_pool, pool_scale, conv_w)


def kernel(x_prompt, x_sample, mem_prompt, state_pool, state_conv, cache_swa_k, cache_swa_v,
           cache_mem_k, cache_mem_v, g_mix_pre, w_in, w_pool, pool_scale, conv_w, swa_sinks,
           g_mem, w_mem_kv, w_out, g_mix_post, g_mlp_pre, w_up, w_down, g_mlp_post):
    depth = w_in.shape[0]
    bp, seq, d_model = x_prompt.shape
    bs, t_new, _ = x_sample.shape
    nmem = mem_prompt.shape[1]
    assert w_in.shape[2] == D_IN and d_model == 4 * D_GROUP

    w_in_b, w_mem_b, w_out_b = w_in.astype(BF16), w_mem_kv.astype(BF16), w_out.astype(BF16)
    w_up_b, w_down_b, w_pool_b = w_up.astype(BF16), w_down.astype(BF16), w_pool.astype(BF16)

    yp = x_prompt.reshape(bp * seq, d_model)
    ys = x_sample.reshape(bs * t_new, d_model)
    mem = mem_prompt.reshape(bp * nmem, d_model)
    kc_all = cache_swa_k.reshape(depth, bs, SWA_WINDOW, SWA_KV_DIM)
    vc_all = cache_swa_v.reshape(depth, bs, SWA_WINDOW, SWA_KV_DIM)
    mkc_all = cache_mem_k.reshape(depth, bs, nmem, D_GROUP)
    mvc_all = cache_mem_v.reshape(depth, bs, nmem, D_GROUP)

    outs = [[] for _ in range(10)]
    for l in range(depth):
        row = lambda a: a[l].reshape(1, -1)
        mixer_w = (w_pool_b[l], row(pool_scale), conv_w[l])

        kv = _norm_matmul(mem, row(g_mem), w_mem_b[l], tm=512)
        mk = kv[:, :D_GROUP].reshape(bp, nmem, D_GROUP)
        mv = kv[:, D_GROUP:].reshape(bp, nmem, D_GROUP)

        proj = _norm_matmul(yp, row(g_mix_pre), w_in_b[l], tm=512).reshape(bp, seq, D_IN)
        mix, pool_p, conv_p, k_p, v_p = _prompt_mixer(swa_sinks[l], proj, mk, mv, *mixer_w, tq=256)
        yp = _matmul_norm_res(mix.reshape(bp * seq, d_model), w_out_b[l], row(g_mix_post), yp, tm=512)
        yp = _mlp(yp, row(g_mlp_pre), w_up_b[l], w_down_b[l], row(g_mlp_post), tm=1024, tf=512)

        proj = _norm_matmul(ys, row(g_mix_pre), w_in_b[l], tm=512).reshape(bs, t_new, D_IN)
        mix, pool_s, conv_s, k_s, v_s = _sample_mixer(
            swa_sinks[l], proj, state_pool[l], state_conv[l], kc_all[l], vc_all[l], mkc_all[l], mvc_all[l],
            *mixer_w, nb=8, pos0=PAST_LEN)
        ys = _matmul_norm_res(mix.reshape(bs * t_new, d_model), w_out_b[l], row(g_mix_post), ys, tm=512)
        ys = _mlp(ys, row(g_mlp_pre), w_up_b[l], w_down_b[l], row(g_mlp_post), tm=512, tf=512)

        kv_shape = lambda a: a.reshape(a.shape[0], SWA_WINDOW, 2, SWA_HEAD_DIM)
        mem_shape = lambda a: a.reshape(bp, nmem, MEM_HEADS, MEM_HEAD_DIM)
        for lst, val in zip(outs, (pool_p, pool_s, conv_p, conv_s, kv_shape(k_p), kv_shape(k_s),
                                   kv_shape(v_p), kv_shape(v_s), mem_shape(mk), mem_shape(mv))):
            lst.append(val)

    return (yp.reshape(bp, seq, d_model), ys.reshape(bs, t_new, d_model)) + tuple(jnp.stack(o) for o in outs)
```

```python
import functools
import math

import jax
import jax.numpy as jnp
from jax import lax
from jax.experimental import pallas as pl
from jax.experimental.pallas import tpu as pltpu

F32 = jnp.float32
BF16 = jnp.bfloat16

LANES = 128
D_GROUP = 512
POOL_WINDOWS = (2, 4, 8, 16)
POOL_CH = D_GROUP // len(POOL_WINDOWS)
POOL_PAD = max(POOL_WINDOWS) - 1
POOL_HALO = 16
CONV_WIDTH = 3
CONV_HALO = 8
SWA_WINDOW = 128
SWA_HEAD_DIM = 64
SWA_Q_HEADS = 8
SWA_GROUP = 4
SWA_KV_DIM = 128
SWA_SCALE = 1.0 / math.sqrt(SWA_HEAD_DIM)
MEM_HEADS = 4
MEM_HEAD_DIM = D_GROUP // MEM_HEADS
MEM_SCALE = 1.0 / math.sqrt(MEM_HEAD_DIM)
RMS_EPS = 1e-6
PAST_LEN = 8192
OFF_U, OFF_HC, OFF_GB, OFF_GC, OFF_Q = 0, 512, 1024, 1536, 2048
OFF_K, OFF_V, OFF_QM, D_IN = 2560, 2688, 2816, 3328
MIX_POOL, MIX_CONV, MIX_SWA, MIX_MEM = 0, 512, 1024, 1536

VMEM_LIMIT_BYTES = 56 * 1024 * 1024


def _rmsnorm(x, g):
    ms = jnp.mean(x * x, axis=-1, keepdims=True)
    return x * lax.rsqrt(ms + RMS_EPS) * g


def _params(*sem):
    return pltpu.CompilerParams(dimension_semantics=sem, vmem_limit_bytes=VMEM_LIMIT_BYTES)


def _layer_block(a, l):
    rest = (0,) * (a.ndim - 1)
    return pl.BlockSpec((None,) + a.shape[1:], lambda *_: (l,) + rest, pipeline_mode=pl.Buffered(1))


def _norm_matmul_kernel(x_ref, g_ref, w_ref, *o_refs):
    h = _rmsnorm(x_ref[...], g_ref[...]).astype(BF16)
    y = jnp.dot(h, w_ref[...], preferred_element_type=F32)
    n = y.shape[1] // len(o_refs)
    for i, o_ref in enumerate(o_refs):
        o_ref[...] = y[:, i * n:(i + 1) * n]


def _norm_matmul(x, g, w, l, tm, n_out=1):
    m, k = x.shape
    n = w.shape[2] // n_out
    outs = pl.pallas_call(
        _norm_matmul_kernel,
        grid=(m // tm,),
        in_specs=[pl.BlockSpec((tm, k), lambda i: (i, 0)), _layer_block(g, l), _layer_block(w, l)],
        out_specs=[pl.BlockSpec((tm, n), lambda i: (i, 0))] * n_out,
        out_shape=[jax.ShapeDtypeStruct((m, n), F32)] * n_out,
        compiler_params=_params("arbitrary"),
        name="norm_matmul",
    )(x, g, w)
    return outs[0] if n_out == 1 else outs


def _matmul_norm_res_kernel(a_ref, w_ref, g_ref, x_ref, o_ref):
    y = jnp.dot(a_ref[...].astype(BF16), w_ref[...], preferred_element_type=F32)
    o_ref[...] = x_ref[...] + _rmsnorm(y, g_ref[...])


def _matmul_norm_res(a, w, g, x, l, tm):
    m, k = a.shape
    n = w.shape[2]
    return pl.pallas_call(
        _matmul_norm_res_kernel,
        grid=(m // tm,),
        in_specs=[pl.BlockSpec((tm, k), lambda i: (i, 0)), _layer_block(w, l), _layer_block(g, l),
                  pl.BlockSpec((tm, n), lambda i: (i, 0))],
        out_specs=pl.BlockSpec((tm, n), lambda i: (i, 0)),
        out_shape=jax.ShapeDtypeStruct((m, n), F32),
        compiler_params=_params("arbitrary"),
        name="matmul_norm_res",
    )(a, w, g, x)


def _mlp_kernel(x_ref, gpre_ref, wup_ref, wdown_ref, gpost_ref, o_ref, h_ref):
    f = pl.program_id(1)

    @pl.when(f == 0)
    def _():
        h_ref[...] = _rmsnorm(x_ref[...], gpre_ref[...]).astype(BF16)
        o_ref[...] = jnp.zeros(o_ref.shape, F32)

    a = jnp.dot(h_ref[...], wup_ref[...], preferred_element_type=F32)
    a = jnp.square(jnp.maximum(a, 0.0)).astype(BF16)
    o_ref[...] += jnp.dot(a, wdown_ref[...], preferred_element_type=F32)

    @pl.when(f == pl.num_programs(1) - 1)
    def _():
        o_ref[...] = x_ref[...] + _rmsnorm(o_ref[...], gpost_ref[...])


def _mlp(x, g_pre, w_up, w_down, g_post, l, tm, tf):
    m, d = x.shape
    ff = w_up.shape[2]
    return pl.pallas_call(
        _mlp_kernel,
        grid=(m // tm, ff // tf),
        in_specs=[pl.BlockSpec((tm, d), lambda i, f: (i, 0)), _layer_block(g_pre, l),
                  pl.BlockSpec((None, d, tf), lambda i, f: (l, 0, f)),
                  pl.BlockSpec((None, tf, d), lambda i, f: (l, f, 0)), _layer_block(g_post, l)],
        out_specs=pl.BlockSpec((tm, d), lambda i, f: (i, 0)),
        out_shape=jax.ShapeDtypeStruct((m, d), F32),
        scratch_shapes=[pltpu.VMEM((tm, d), BF16)],
        compiler_params=_params("arbitrary", "arbitrary"),
        name="mlp",
    )(x, g_pre, w_up, w_down, g_post)


def _pool_delta(uext_ref, u, pos0):
    rows = u.shape[0]
    uext_ref[POOL_HALO:POOL_HALO + rows, :] = u
    pos = pos0 + lax.broadcasted_iota(jnp.int32, (rows, POOL_CH), 0)
    outs = []
    for gi, w in enumerate(POOL_WINDOWS):
        c0 = gi * POOL_CH
        s = u[:, c0:c0 + POOL_CH]
        for back in range(1, w):
            s = s + uext_ref[POOL_HALO - back:POOL_HALO - back + rows, c0:c0 + POOL_CH]
        cnt = jnp.minimum(pos + 1, w).astype(F32)
        outs.append(s / cnt - u[:, c0:c0 + POOL_CH])
    return outs


def _pool_project(deltas, wpool_ref, pscale_ref):
    outs = []
    for gi, d in enumerate(deltas):
        y = jnp.dot(d.astype(BF16), wpool_ref[gi], preferred_element_type=F32)
        outs.append(y * pscale_ref[:, gi * POOL_CH:(gi + 1) * POOL_CH])
    return outs


def _conv_mixer(cext_ref, hc, gb, gc, convw_ref):
    rows = hc.shape[0]
    cext_ref[CONV_HALO:CONV_HALO + rows, :] = gc * hc
    conv = convw_ref[0:1, :] * cext_ref[CONV_HALO - 2:CONV_HALO - 2 + rows, :]
    for kk in range(1, CONV_WIDTH):
        lo = CONV_HALO - 2 + kk
        conv = conv + convw_ref[kk:kk + 1, :] * cext_ref[lo:lo + rows, :]
    return gb * conv


def _swa_block(q, kwin, vwin, sinks_ref, l, first_key):
    rows = q.shape[0]
    nk = kwin.shape[0]
    low = lax.broadcasted_iota(jnp.int32, (nk, LANES), 1) < SWA_HEAD_DIM
    kroll = pltpu.roll(kwin, SWA_HEAD_DIM, 1)
    vroll = pltpu.roll(vwin, SWA_HEAD_DIM, 1)
    ind_a = jnp.where(low, 1.0, 0.0)
    ind_b = 1.0 - ind_a
    qi = lax.broadcasted_iota(jnp.int32, (rows, 2 * nk), 0)
    kj = lax.broadcasted_iota(jnp.int32, (rows, 2 * nk), 1) & (nk - 1)
    valid = (kj > qi) & (kj <= qi + SWA_WINDOW) & (kj >= first_key)
    bias = jnp.where(valid, 0.0, -jnp.inf)
    low_q = lax.broadcasted_iota(jnp.int32, (rows, LANES), 1) < SWA_HEAD_DIM
    outs = []
    for h in range(2):
        if h == 0:
            k_a, k_b = jnp.where(low, kwin, 0.0), jnp.where(low, 0.0, kroll)
            v_a, v_b = jnp.where(low, vwin, 0.0), jnp.where(low, 0.0, vroll)
        else:
            k_a, k_b = jnp.where(low, kroll, 0.0), jnp.where(low, 0.0, kwin)
            v_a, v_b = jnp.where(low, vroll, 0.0), jnp.where(low, 0.0, vwin)
        kst = jnp.concatenate([k_a, k_b], axis=0).astype(BF16)
        vst = jnp.concatenate([jnp.concatenate([v_a, ind_a], axis=1),
                               jnp.concatenate([v_b, ind_b], axis=1)], axis=0).astype(BF16)
        for r in range(2):
            pair = 2 * h + r
            qp = (q[:, pair * LANES:(pair + 1) * LANES] * SWA_SCALE).astype(BF16)
            s = lax.dot_general(qp, kst, (((1,), (1,)), ((), ())),
                                preferred_element_type=F32) + bias
            sink_a = sinks_ref[l, 4 * h + 2 * r]
            sink_b = sinks_ref[l, 4 * h + 2 * r + 1]
            m_a = jnp.maximum(jnp.max(s[:, :nk], axis=1, keepdims=True), sink_a)
            m_b = jnp.maximum(jnp.max(s[:, nk:], axis=1, keepdims=True), sink_b)
            p = jnp.concatenate([jnp.exp(s[:, :nk] - m_a), jnp.exp(s[:, nk:] - m_b)],
                                axis=1).astype(BF16)
            o = jnp.dot(p, vst, preferred_element_type=F32)
            den = o[:, LANES:] + jnp.where(low_q, jnp.exp(sink_a - m_a), jnp.exp(sink_b - m_b))
            outs.append(o[:, :LANES] / den)
    return outs


def _mem_block(qm, mk, mv):
    ones = jnp.ones((mk.shape[0], LANES), BF16)
    outs = []
    for hm in range(MEM_HEADS):
        c0 = hm * MEM_HEAD_DIM
        s = lax.dot_general(qm[:, c0:c0 + MEM_HEAD_DIM].astype(BF16), mk[:, c0:c0 + MEM_HEAD_DIM],
                            (((1,), (1,)), ((), ())), preferred_element_type=F32) * MEM_SCALE
        m = jnp.max(s, axis=1, keepdims=True)
        p = jnp.exp(s - m).astype(BF16)
        vext = jnp.concatenate([mv[:, c0:c0 + MEM_HEAD_DIM], ones], axis=1)
        o = jnp.dot(p, vext, preferred_element_type=F32)
        outs.append(o[:, :MEM_HEAD_DIM] / o[:, MEM_HEAD_DIM:])
    return outs


def _prompt_mixer_kernel(sinks_ref, proj_ref, mk_ref, mv_ref, wpool_ref, pscale_ref, convw_ref,
                         mix_ref, npool_ref, nconv_ref, nk_ref, nv_ref,
                         uext_ref, cext_ref, kext_ref, vext_ref, *, l, tq):
    j = pl.program_id(1)

    @pl.when(j == 0)
    def _():
        uext_ref[0:POOL_HALO, :] = jnp.zeros((POOL_HALO, D_GROUP), F32)
        cext_ref[0:CONV_HALO, :] = jnp.zeros((CONV_HALO, D_GROUP), F32)
        kext_ref[0:SWA_WINDOW, :] = jnp.zeros((SWA_WINDOW, SWA_KV_DIM), F32)
        vext_ref[0:SWA_WINDOW, :] = jnp.zeros((SWA_WINDOW, SWA_KV_DIM), F32)

    u = proj_ref[:, OFF_U:OFF_U + D_GROUP]
    for gi, y in enumerate(_pool_project(_pool_delta(uext_ref, u, j * tq), wpool_ref, pscale_ref)):
        mix_ref[:, MIX_POOL + gi * POOL_CH:MIX_POOL + (gi + 1) * POOL_CH] = y.astype(mix_ref.dtype)
    npool_ref[...] = uext_ref[tq + POOL_HALO - POOL_PAD:tq + POOL_HALO, :]
    uext_ref[0:POOL_HALO, :] = uext_ref[tq:tq + POOL_HALO, :]

    y = _conv_mixer(cext_ref, proj_ref[:, OFF_HC:OFF_HC + D_GROUP], proj_ref[:, OFF_GB:OFF_GB + D_GROUP],
                    proj_ref[:, OFF_GC:OFF_GC + D_GROUP], convw_ref)
    mix_ref[:, MIX_CONV:MIX_CONV + D_GROUP] = y.astype(mix_ref.dtype)
    nconv_ref[...] = cext_ref[tq + CONV_HALO - 2:tq + CONV_HALO, :]
    cext_ref[0:CONV_HALO, :] = cext_ref[tq:tq + CONV_HALO, :]

    kext_ref[SWA_WINDOW:SWA_WINDOW + tq, :] = proj_ref[:, OFF_K:OFF_K + SWA_KV_DIM]
    vext_ref[SWA_WINDOW:SWA_WINDOW + tq, :] = proj_ref[:, OFF_V:OFF_V + SWA_KV_DIM]
    for sb in range(tq // SWA_WINDOW):
        r0 = sb * SWA_WINDOW
        first_key = jnp.maximum(SWA_WINDOW - (j * tq + r0), 0)
        outs = _swa_block(proj_ref[r0:r0 + SWA_WINDOW, OFF_Q:OFF_Q + D_GROUP],
                          kext_ref[r0:r0 + 2 * SWA_WINDOW, :], vext_ref[r0:r0 + 2 * SWA_WINDOW, :],
                          sinks_ref, l, first_key)
        for pair, y in enumerate(outs):
            c0 = MIX_SWA + pair * LANES
            mix_ref[r0:r0 + SWA_WINDOW, c0:c0 + LANES] = y.astype(mix_ref.dtype)
    nk_ref[...] = kext_ref[tq:tq + SWA_WINDOW, :]
    nv_ref[...] = vext_ref[tq:tq + SWA_WINDOW, :]
    kext_ref[0:SWA_WINDOW, :] = kext_ref[tq:tq + SWA_WINDOW, :]
    vext_ref[0:SWA_WINDOW, :] = vext_ref[tq:tq + SWA_WINDOW, :]

    outs = _mem_block(proj_ref[:, OFF_QM:OFF_QM + D_GROUP], mk_ref[...].astype(BF16), mv_ref[...].astype(BF16))
    for hm, y in enumerate(outs):
        c0 = MIX_MEM + hm * MEM_HEAD_DIM
        mix_ref[:, c0:c0 + MEM_HEAD_DIM] = y.astype(mix_ref.dtype)


def _prompt_mixer(sinks, proj, mk, mv, w_pool, pool_scale, conv_w, l, tq):
    b, t, _ = proj.shape
    nmem = mk.shape[1]
    per_seq = lambda i, j: (i, 0, 0)
    return pl.pallas_call(
        functools.partial(_prompt_mixer_kernel, l=l, tq=tq),
        grid=(b, t // tq),
        in_specs=[pl.BlockSpec(memory_space=pltpu.SMEM),
                  pl.BlockSpec((None, tq, D_IN), lambda i, j: (i, j, 0)),
                  pl.BlockSpec((None, nmem, D_GROUP), per_seq),
                  pl.BlockSpec((None, nmem, D_GROUP), per_seq),
                  _layer_block(w_pool, l), _layer_block(pool_scale, l), _layer_block(conv_w, l)],
        out_specs=[pl.BlockSpec((None, tq, 4 * D_GROUP), lambda i, j: (i, j, 0)),
                   pl.BlockSpec((None, POOL_PAD, D_GROUP), per_seq),
                   pl.BlockSpec((None, CONV_WIDTH - 1, D_GROUP), per_seq),
                   pl.BlockSpec((None, SWA_WINDOW, SWA_KV_DIM), per_seq),
                   pl.BlockSpec((None, SWA_WINDOW, SWA_KV_DIM), per_seq)],
        out_shape=[jax.ShapeDtypeStruct((b, t, 4 * D_GROUP), BF16),
                   jax.ShapeDtypeStruct((b, POOL_PAD, D_GROUP), F32),
                   jax.ShapeDtypeStruct((b, CONV_WIDTH - 1, D_GROUP), F32),
                   jax.ShapeDtypeStruct((b, SWA_WINDOW, SWA_KV_DIM), F32),
                   jax.ShapeDtypeStruct((b, SWA_WINDOW, SWA_KV_DIM), F32)],
        scratch_shapes=[pltpu.VMEM((POOL_HALO + tq, D_GROUP), F32),
                        pltpu.VMEM((CONV_HALO + tq, D_GROUP), F32),
                        pltpu.VMEM((SWA_WINDOW + tq, SWA_KV_DIM), F32),
                        pltpu.VMEM((SWA_WINDOW + tq, SWA_KV_DIM), F32)],
        compiler_params=_params("arbitrary", "arbitrary"),
        name="prompt_mixer",
    )(sinks, proj, mk, mv, w_pool, pool_scale, conv_w)


def _sample_mixer_kernel(sinks_ref, proj_ref, pool_ref, conv_ref, kc_ref, vc_ref, mk_ref, mv_ref,
                         wpool_ref, pscale_ref, convw_ref,
                         mix_ref, npool_ref, nconv_ref, nk_ref, nv_ref,
                         uext_all, cext_all, kext_all, vext_all, qs_all, qm_all,
                         *, l, nb, t_new, pos0):
    uext_all[...] = jnp.zeros(uext_all.shape, F32)
    cext_all[...] = jnp.zeros(cext_all.shape, F32)
    kext_all[...] = jnp.zeros(kext_all.shape, F32)
    vext_all[...] = jnp.zeros(vext_all.shape, F32)

    n_swa = SWA_Q_HEADS * t_new
    n_mem = MEM_HEADS * t_new
    low = lax.broadcasted_iota(jnp.int32, (t_new, LANES), 1) < SWA_HEAD_DIM
    rows_of = lambda a: slice(a * t_new, (a + 1) * t_new)

    deltas = []
    for b in range(nb):
        r = rows_of(b)
        uext_ref, cext_ref, kext_ref, vext_ref = uext_all.at[b], cext_all.at[b], kext_all.at[b], vext_all.at[b]
        uext_ref[POOL_HALO - POOL_PAD:POOL_HALO, :] = pool_ref[b]
        deltas.append(_pool_delta(uext_ref, proj_ref[r, OFF_U:OFF_U + D_GROUP], pos0))
        npool_ref[b] = uext_ref[t_new + POOL_HALO - POOL_PAD:t_new + POOL_HALO, :]

        cext_ref[CONV_HALO - 2:CONV_HALO, :] = conv_ref[b]
        mix_ref[r, MIX_CONV:MIX_CONV + D_GROUP] = _conv_mixer(
            cext_ref, proj_ref[r, OFF_HC:OFF_HC + D_GROUP], proj_ref[r, OFF_GB:OFF_GB + D_GROUP],
            proj_ref[r, OFF_GC:OFF_GC + D_GROUP], convw_ref)
        nconv_ref[b] = cext_ref[t_new + CONV_HALO - 2:t_new + CONV_HALO, :]

        kext_ref[0:SWA_WINDOW, :] = kc_ref[b]
        vext_ref[0:SWA_WINDOW, :] = vc_ref[b]
        kext_ref[SWA_WINDOW:SWA_WINDOW + t_new, :] = proj_ref[r, OFF_K:OFF_K + SWA_KV_DIM]
        vext_ref[SWA_WINDOW:SWA_WINDOW + t_new, :] = proj_ref[r, OFF_V:OFF_V + SWA_KV_DIM]
        nk_ref[b] = kext_ref[t_new:t_new + SWA_WINDOW, :]
        nv_ref[b] = vext_ref[t_new:t_new + SWA_WINDOW, :]
        for pair in range(SWA_Q_HEADS // 2):
            qp = proj_ref[r, OFF_Q + pair * LANES:OFF_Q + (pair + 1) * LANES] * SWA_SCALE
            qr = pltpu.roll(qp, SWA_HEAD_DIM, 1)
            if pair // 2 == 0:
                q_even, q_odd = jnp.where(low, qp, 0.0), jnp.where(low, qr, 0.0)
            else:
                q_even, q_odd = jnp.where(low, 0.0, qr), jnp.where(low, 0.0, qp)
            qs_all[b, rows_of(2 * pair), :] = q_even
            qs_all[b, rows_of(2 * pair + 1), :] = q_odd
        for hm in range(MEM_HEADS):
            qm_all[b, rows_of(hm), :] = (
                proj_ref[r, OFF_QM + hm * MEM_HEAD_DIM:OFF_QM + (hm + 1) * MEM_HEAD_DIM])

    deltas = [jnp.concatenate([d[gi] for d in deltas], axis=0) for gi in range(len(POOL_WINDOWS))]
    for gi, y in enumerate(_pool_project(deltas, wpool_ref, pscale_ref)):
        mix_ref[:, MIX_POOL + gi * POOL_CH:MIX_POOL + (gi + 1) * POOL_CH] = y

    nt_dims = (((1,), (1,)), ((), ()))
    s_swa = jnp.concatenate(
        [lax.dot_general(qs_all[b].astype(BF16), kext_all[b].astype(BF16),
                         nt_dims, preferred_element_type=F32) for b in range(nb)], axis=0)
    s_mem = jnp.concatenate(
        [lax.dot_general(qm_all[b].astype(BF16), mk_ref[b].astype(BF16),
                         nt_dims, preferred_element_type=F32) for b in range(nb)], axis=0)

    ri = lax.broadcasted_iota(jnp.int32, s_swa.shape, 0)
    tok = ri % t_new
    kj = lax.broadcasted_iota(jnp.int32, s_swa.shape, 1)
    valid = (kj > tok) & (kj <= tok + SWA_WINDOW) & (kj >= max(SWA_WINDOW - pos0, 0))
    head = (lax.broadcasted_iota(jnp.int32, (s_swa.shape[0], 1), 0) // t_new) % SWA_Q_HEADS
    sink = jnp.zeros(head.shape, F32)
    for hq in range(SWA_Q_HEADS):
        sink = jnp.where(head == hq, sinks_ref[l, hq], sink)
    s_swa = jnp.where(valid, s_swa, -jnp.inf)
    m = jnp.maximum(jnp.max(s_swa, axis=1, keepdims=True), sink)
    e = jnp.exp(s_swa - m)
    p_swa = (e * (1.0 / (jnp.sum(e, axis=1, keepdims=True) + jnp.exp(sink - m)))).astype(BF16)

    mem_valid = (lax.broadcasted_iota(jnp.int32, s_mem.shape, 1) % MEM_HEADS
                 == (lax.broadcasted_iota(jnp.int32, s_mem.shape, 0) // t_new) % MEM_HEADS)
    s_mem = jnp.where(mem_valid, s_mem * MEM_SCALE, -jnp.inf)
    e = jnp.exp(s_mem - jnp.max(s_mem, axis=1, keepdims=True))
    p_mem = (e * (1.0 / jnp.sum(e, axis=1, keepdims=True))).astype(BF16)

    for b in range(nb):
        r = rows_of(b)
        o = jnp.dot(p_swa[b * n_swa:(b + 1) * n_swa, :], vext_all[b].astype(BF16), preferred_element_type=F32)
        o_roll = pltpu.roll(o, SWA_HEAD_DIM, 1)
        for pair in range(SWA_Q_HEADS // 2):
            src_even, src_odd = (o, o_roll) if pair // 2 == 0 else (o_roll, o)
            mix_ref[r, MIX_SWA + pair * LANES:MIX_SWA + (pair + 1) * LANES] = jnp.where(
                low, src_even[rows_of(2 * pair), :], src_odd[rows_of(2 * pair + 1), :])
        o = jnp.dot(p_mem[b * n_mem:(b + 1) * n_mem, :], mv_ref[b].astype(BF16), preferred_element_type=F32)
        for hm in range(MEM_HEADS):
            mix_ref[r, MIX_MEM + hm * MEM_HEAD_DIM:MIX_MEM + (hm + 1) * MEM_HEAD_DIM] = o[rows_of(hm), :]


def _sample_mixer(sinks, proj, pool, conv, kc, vc, mk, mv, w_pool, pool_scale, conv_w, l, nb, t_new, pos0):
    b = proj.shape[0] // t_new
    slots = nb
    blk = lambda a: pl.BlockSpec((None, nb) + a.shape[2:], lambda i: (l, i) + (0,) * (a.ndim - 2))
    out_blk = lambda a: pl.BlockSpec((nb,) + a.shape[2:], lambda i: (i,) + (0,) * (a.ndim - 2))
    out_sds = lambda a: jax.ShapeDtypeStruct(a.shape[1:], F32)
    return pl.pallas_call(
        functools.partial(_sample_mixer_kernel, l=l, nb=nb, t_new=t_new, pos0=pos0),
        grid=(b // nb,),
        in_specs=[pl.BlockSpec(memory_space=pltpu.SMEM),
                  pl.BlockSpec((nb * t_new, D_IN), lambda i: (i, 0)),
                  blk(pool), blk(conv), blk(kc), blk(vc), blk(mk), blk(mv),
                  _layer_block(w_pool, l), _layer_block(pool_scale, l), _layer_block(conv_w, l)],
        out_specs=[pl.BlockSpec((nb * t_new, 4 * D_GROUP), lambda i: (i, 0)),
                   out_blk(pool), out_blk(conv), out_blk(kc), out_blk(vc)],
        out_shape=[jax.ShapeDtypeStruct((b * t_new, 4 * D_GROUP), F32),
                   out_sds(pool), out_sds(conv), out_sds(kc), out_sds(vc)],
        scratch_shapes=[pltpu.VMEM((slots, POOL_HALO + 8, D_GROUP), F32),
                        pltpu.VMEM((slots, CONV_HALO + 8, D_GROUP), F32),
                        pltpu.VMEM((slots, 2 * SWA_WINDOW, SWA_KV_DIM), F32),
                        pltpu.VMEM((slots, 2 * SWA_WINDOW, SWA_KV_DIM), F32),
                        pltpu.VMEM((slots, SWA_Q_HEADS * t_new, LANES), F32),
                        pltpu.VMEM((slots, MEM_HEADS * t_new, MEM_HEAD_DIM), F32)],
        compiler_params=_params("arbitrary"),
        name="sample_mixer",
    )(sinks, proj, pool, conv, kc, vc, mk, mv, w_pool, pool_scale, conv_w)


def kernel(x_prompt, x_sample, mem_prompt, state_pool, state_conv, cache_swa_k, cache_swa_v,
           cache_mem_k, cache_mem_v, g_mix_pre, w_in, w_pool, pool_scale, conv_w, swa_sinks,
           g_mem, w_mem_kv, w_out, g_mix_post, g_mlp_pre, w_up, w_down, g_mlp_post):
    depth = w_in.shape[0]
    bp, seq, d_model = x_prompt.shape
    bs, t_new, _ = x_sample.shape
    nmem = mem_prompt.shape[1]
    assert w_in.shape[2] == D_IN and d_model == 4 * D_GROUP

    w_in_b, w_mem_b, w_out_b = w_in.astype(BF16), w_mem_kv.astype(BF16), w_out.astype(BF16)
    w_up_b, w_down_b, w_pool_b = w_up.astype(BF16), w_down.astype(BF16), w_pool.astype(BF16)
    rows = lambda a: a.reshape(depth, 1, a.shape[-1])
    g_mix_pre, g_mem, g_mix_post = rows(g_mix_pre), rows(g_mem), rows(g_mix_post)
    g_mlp_pre, g_mlp_post, pool_scale = rows(g_mlp_pre), rows(g_mlp_post), rows(pool_scale)

    yp = x_prompt.reshape(bp * seq, d_model)
    ys = x_sample.reshape(bs * t_new, d_model)
    mem = mem_prompt.reshape(bp * nmem, d_model)
    kc = cache_swa_k.reshape(depth, bs, SWA_WINDOW, SWA_KV_DIM)
    vc = cache_swa_v.reshape(depth, bs, SWA_WINDOW, SWA_KV_DIM)
    mkc = cache_mem_k.reshape(depth, bs, nmem * MEM_HEADS, MEM_HEAD_DIM)
    mvc = cache_mem_v.reshape(depth, bs, nmem * MEM_HEADS, MEM_HEAD_DIM)

    outs = [[] for _ in range(10)]
    for l in range(depth):
        mk, mv = _norm_matmul(mem, g_mem, w_mem_b, l, tm=512, n_out=2)
        mk, mv = mk.reshape(bp, nmem, D_GROUP), mv.reshape(bp, nmem, D_GROUP)

        proj = _norm_matmul(yp, g_mix_pre, w_in_b, l, tm=512).reshape(bp, seq, D_IN)
        mix, pool_p, conv_p, k_p, v_p = _prompt_mixer(swa_sinks, proj, mk, mv, w_pool_b, pool_scale, conv_w,
                                                      l, tq=256)
        yp = _matmul_norm_res(mix.reshape(bp * seq, d_model), w_out_b, g_mix_post, yp, l, tm=512)
        yp = _mlp(yp, g_mlp_pre, w_up_b, w_down_b, g_mlp_post, l, tm=1024, tf=512)

        proj = _norm_matmul(ys, g_mix_pre, w_in_b, l, tm=512)
        mix, pool_s, conv_s, k_s, v_s = _sample_mixer(
            swa_sinks, proj, state_pool, state_conv, kc, vc, mkc, mvc, w_pool_b, pool_scale, conv_w,
            l, nb=8, t_new=t_new, pos0=PAST_LEN)
        ys = _matmul_norm_res(mix, w_out_b, g_mix_post, ys, l, tm=512)
        ys = _mlp(ys, g_mlp_pre, w_up_b, w_down_b, g_mlp_post, l, tm=512, tf=512)

        kv_shape = lambda a: a.reshape(a.shape[0], SWA_WINDOW, 2, SWA_HEAD_DIM)
        mem_shape = lambda a: a.reshape(bp, nmem, MEM_HEADS, MEM_HEAD_DIM)
        for lst, val in zip(outs, (pool_p, pool_s, conv_p, conv_s, kv_shape(k_p), kv_shape(k_s),
                                   kv_shape(v_p), kv_shape(v_s), mem_shape(mk), mem_shape(mv))):
            lst.append(val)

    return (yp.reshape(bp, seq, d_model), ys.reshape(bs, t_new, d_model)) + tuple(jnp.stack(o) for o in outs)
```

```python
import functools
import math

import jax
import jax.numpy as jnp
from jax import lax
from jax.experimental import pallas as pl
from jax.experimental.pallas import tpu as pltpu

F32 = jnp.float32
BF16 = jnp.bfloat16

LANES = 128
D_GROUP = 512
POOL_WINDOWS = (2, 4, 8, 16)
POOL_CH = D_GROUP // len(POOL_WINDOWS)
POOL_PAD = max(POOL_WINDOWS) - 1
POOL_HALO = 16
CONV_WIDTH = 3
CONV_HALO = 8
SWA_WINDOW = 128
SWA_HEAD_DIM = 64
SWA_Q_HEADS = 8
SWA_GROUP = 4
SWA_KV_DIM = 128
SWA_SCALE = 1.0 / math.sqrt(SWA_HEAD_DIM)
MEM_HEADS = 4
MEM_HEAD_DIM = D_GROUP // MEM_HEADS
MEM_SCALE = 1.0 / math.sqrt(MEM_HEAD_DIM)
RMS_EPS = 1e-6
PAST_LEN = 8192
OFF_U, OFF_HC, OFF_GB, OFF_GC, OFF_Q = 0, 512, 1024, 1536, 2048
OFF_K, OFF_V, OFF_QM, D_IN = 2560, 2688, 2816, 3328
MIX_POOL, MIX_CONV, MIX_SWA, MIX_MEM = 0, 512, 1024, 1536

VMEM_LIMIT_BYTES = 56 * 1024 * 1024


def _rmsnorm(x, g):
    ms = jnp.mean(x * x, axis=-1, keepdims=True)
    return x * lax.rsqrt(ms + RMS_EPS) * g


def _params(*sem):
    return pltpu.CompilerParams(dimension_semantics=sem, vmem_limit_bytes=VMEM_LIMIT_BYTES)


def _layer_block(a, l):
    rest = (0,) * (a.ndim - 1)
    return pl.BlockSpec((None,) + a.shape[1:], lambda *_: (l,) + rest, pipeline_mode=pl.Buffered(1))


def _norm_matmul_kernel(x_ref, g_ref, w_ref, *o_refs):
    h = _rmsnorm(x_ref[...], g_ref[...]).astype(BF16)
    y = jnp.dot(h, w_ref[...], preferred_element_type=F32)
    n = y.shape[1] // len(o_refs)
    for i, o_ref in enumerate(o_refs):
        o_ref[...] = y[:, i * n:(i + 1) * n]


def _norm_matmul(x, g, w, l, tm, n_out=1):
    m, k = x.shape
    n = w.shape[2] // n_out
    outs = pl.pallas_call(
        _norm_matmul_kernel,
        grid=(m // tm,),
        in_specs=[pl.BlockSpec((tm, k), lambda i: (i, 0)), _layer_block(g, l), _layer_block(w, l)],
        out_specs=[pl.BlockSpec((tm, n), lambda i: (i, 0))] * n_out,
        out_shape=[jax.ShapeDtypeStruct((m, n), F32)] * n_out,
        compiler_params=_params("arbitrary"),
        name="norm_matmul",
    )(x, g, w)
    return outs[0] if n_out == 1 else outs


def _matmul_norm_res_kernel(a_ref, w_ref, g_ref, x_ref, o_ref):
    y = jnp.dot(a_ref[...].astype(BF16), w_ref[...], preferred_element_type=F32)
    o_ref[...] = x_ref[...] + _rmsnorm(y, g_ref[...])


def _matmul_norm_res(a, w, g, x, l, tm):
    m, k = a.shape
    n = w.shape[2]
    return pl.pallas_call(
        _matmul_norm_res_kernel,
        grid=(m // tm,),
        in_specs=[pl.BlockSpec((tm, k), lambda i: (i, 0)), _layer_block(w, l), _layer_block(g, l),
                  pl.BlockSpec((tm, n), lambda i: (i, 0))],
        out_specs=pl.BlockSpec((tm, n), lambda i: (i, 0)),
        out_shape=jax.ShapeDtypeStruct((m, n), F32),
        compiler_params=_params("arbitrary"),
        name="matmul_norm_res",
    )(a, w, g, x)


def _mlp_kernel(x_ref, gpre_ref, wup_ref, wdown_ref, gpost_ref, o_ref, *rest, cast_weights):
    f = pl.program_id(1)
    h_ref = rest[-1]

    @pl.when(f == 0)
    def _():
        h_ref[...] = _rmsnorm(x_ref[...], gpre_ref[...]).astype(BF16)
        o_ref[...] = jnp.zeros(o_ref.shape, F32)

    if cast_weights:
        wup_b_ref, wdown_b_ref = rest[:2]
        wup, wdown = wup_ref[...].astype(BF16), wdown_ref[...].astype(BF16)
        wup_b_ref[...] = wup
        wdown_b_ref[...] = wdown
    else:
        wup, wdown = wup_ref[...], wdown_ref[...]
    a = jnp.dot(h_ref[...], wup, preferred_element_type=F32)
    a = jnp.square(jnp.maximum(a, 0.0)).astype(BF16)
    o_ref[...] += jnp.dot(a, wdown, preferred_element_type=F32)

    @pl.when(f == pl.num_programs(1) - 1)
    def _():
        o_ref[...] = x_ref[...] + _rmsnorm(o_ref[...], gpost_ref[...])


def _mlp(x, g_pre, w_up, w_down, g_post, l, tm, tf):
    m, d = x.shape
    cast_weights = w_up.ndim == 3
    ff = w_up.shape[-1]
    if cast_weights:
        assert m == tm, "each weight block must be visited exactly once"
        w_specs = [pl.BlockSpec((None, d, tf), lambda i, f: (l, 0, f)),
                   pl.BlockSpec((None, tf, d), lambda i, f: (l, f, 0))]
    else:
        w_specs = [pl.BlockSpec((d, tf), lambda i, f: (0, f)), pl.BlockSpec((tf, d), lambda i, f: (f, 0))]
    out_specs = [pl.BlockSpec((tm, d), lambda i, f: (i, 0))]
    out_shape = [jax.ShapeDtypeStruct((m, d), F32)]
    if cast_weights:
        out_specs += [pl.BlockSpec((d, tf), lambda i, f: (0, f)), pl.BlockSpec((tf, d), lambda i, f: (f, 0))]
        out_shape += [jax.ShapeDtypeStruct((d, ff), BF16), jax.ShapeDtypeStruct((ff, d), BF16)]
    outs = pl.pallas_call(
        functools.partial(_mlp_kernel, cast_weights=cast_weights),
        grid=(m // tm, ff // tf),
        in_specs=[pl.BlockSpec((tm, d), lambda i, f: (i, 0)), _layer_block(g_pre, l)] + w_specs
                 + [_layer_block(g_post, l)],
        out_specs=out_specs,
        out_shape=out_shape,
        scratch_shapes=[pltpu.VMEM((tm, d), BF16)],
        compiler_params=_params("arbitrary", "arbitrary"),
        name="mlp",
    )(x, g_pre, w_up, w_down, g_post)
    return outs if cast_weights else outs[0]


def _pool_delta(uext_ref, u, pos0):
    rows = u.shape[0]
    uext_ref[POOL_HALO:POOL_HALO + rows, :] = u
    pos = pos0 + lax.broadcasted_iota(jnp.int32, (rows, POOL_CH), 0)
    outs = []
    for gi, w in enumerate(POOL_WINDOWS):
        c0 = gi * POOL_CH
        s = u[:, c0:c0 + POOL_CH]
        for back in range(1, w):
            s = s + uext_ref[POOL_HALO - back:POOL_HALO - back + rows, c0:c0 + POOL_CH]
        cnt = jnp.minimum(pos + 1, w).astype(F32)
        outs.append(s / cnt - u[:, c0:c0 + POOL_CH])
    return outs


def _pool_project(deltas, wpool_ref, pscale_ref):
    outs = []
    for gi, d in enumerate(deltas):
        y = jnp.dot(d.astype(BF16), wpool_ref[gi], preferred_element_type=F32)
        outs.append(y * pscale_ref[:, gi * POOL_CH:(gi + 1) * POOL_CH])
    return outs


def _conv_mixer(cext_ref, hc, gb, gc, convw_ref):
    rows = hc.shape[0]
    cext_ref[CONV_HALO:CONV_HALO + rows, :] = gc * hc
    conv = convw_ref[0:1, :] * cext_ref[CONV_HALO - 2:CONV_HALO - 2 + rows, :]
    for kk in range(1, CONV_WIDTH):
        lo = CONV_HALO - 2 + kk
        conv = conv + convw_ref[kk:kk + 1, :] * cext_ref[lo:lo + rows, :]
    return gb * conv


def _swa_kv_head(h, q_of_pair, kwin, vwin, sinks_ref, l, first_key):
    nk = kwin.shape[0]
    low = lax.broadcasted_iota(jnp.int32, (nk, LANES), 1) < SWA_HEAD_DIM
    kroll = pltpu.roll(kwin, SWA_HEAD_DIM, 1)
    vroll = pltpu.roll(vwin, SWA_HEAD_DIM, 1)
    ind_a = jnp.where(low, 1.0, 0.0)
    ind_b = 1.0 - ind_a
    if h == 0:
        k_a, k_b = jnp.where(low, kwin, 0.0), jnp.where(low, 0.0, kroll)
        v_a, v_b = jnp.where(low, vwin, 0.0), jnp.where(low, 0.0, vroll)
    else:
        k_a, k_b = jnp.where(low, kroll, 0.0), jnp.where(low, 0.0, kwin)
        v_a, v_b = jnp.where(low, vroll, 0.0), jnp.where(low, 0.0, vwin)
    kst = jnp.concatenate([k_a, k_b], axis=0).astype(BF16)
    vst = jnp.concatenate([jnp.concatenate([v_a, ind_a], axis=1),
                           jnp.concatenate([v_b, ind_b], axis=1)], axis=0).astype(BF16)
    outs = {}
    for r in range(2):
        pair = 2 * h + r
        q = q_of_pair(pair)
        rows = q.shape[0]
        qi = lax.broadcasted_iota(jnp.int32, (rows, 2 * nk), 0)
        kj = lax.broadcasted_iota(jnp.int32, (rows, 2 * nk), 1) & (nk - 1)
        valid = (kj > qi) & (kj <= qi + SWA_WINDOW) & (kj >= first_key)
        s = lax.dot_general((q * SWA_SCALE).astype(BF16), kst, (((1,), (1,)), ((), ())),
                            preferred_element_type=F32)
        s = jnp.where(valid, s, -jnp.inf)
        sink_a = sinks_ref[l, 4 * h + 2 * r]
        sink_b = sinks_ref[l, 4 * h + 2 * r + 1]
        m_a = jnp.maximum(jnp.max(s[:, :nk], axis=1, keepdims=True), sink_a)
        m_b = jnp.maximum(jnp.max(s[:, nk:], axis=1, keepdims=True), sink_b)
        p = jnp.concatenate([jnp.exp(s[:, :nk] - m_a), jnp.exp(s[:, nk:] - m_b)],
                            axis=1).astype(BF16)
        o = jnp.dot(p, vst, preferred_element_type=F32)
        low_q = lax.broadcasted_iota(jnp.int32, (rows, LANES), 1) < SWA_HEAD_DIM
        den = o[:, LANES:] + jnp.where(low_q, jnp.exp(sink_a - m_a), jnp.exp(sink_b - m_b))
        outs[pair] = o[:, :LANES] / den
    return outs


def _mem_head(qm_h, mk_h, mv_h):
    s = lax.dot_general(qm_h.astype(BF16), mk_h, (((1,), (1,)), ((), ())),
                        preferred_element_type=F32) * MEM_SCALE
    p = jnp.exp(s - jnp.max(s, axis=1, keepdims=True)).astype(BF16)
    vext = jnp.concatenate([mv_h, jnp.ones(mv_h.shape, BF16)], axis=1)
    o = jnp.dot(p, vext, preferred_element_type=F32)
    return o[:, :MEM_HEAD_DIM] / o[:, MEM_HEAD_DIM:]


PROJ_CHUNK = 256


def _prompt_layer_kernel(sinks_ref, xc_ref, xn_ref, gpre_ref, win_ref, mk_ref, mv_ref, wpool_ref, pscale_ref,
                         convw_ref, wout_ref, gpost_ref,
                         o_ref, npool_ref, nconv_ref, nk_ref, nv_ref,
                         proj_ref, h_ref, mix_ref, uext_ref, cext_ref, kext_ref, vext_ref, *, l, tq, nblk):
    r = pl.program_id(0)
    j = r % nblk
    cur = r % 2

    def project_chunk(slot, c):
        cols = slice(c * PROJ_CHUNK, (c + 1) * PROJ_CHUNK)
        proj_ref[slot, :, cols] = jnp.dot(h_ref[...], win_ref[:, cols], preferred_element_type=F32)

    @pl.when(r == 0)
    def _():
        h_ref[...] = _rmsnorm(xc_ref[...], gpre_ref[...]).astype(BF16)
        for c in range(D_IN // PROJ_CHUNK):
            project_chunk(0, c)

    @pl.when(j == 0)
    def _():
        uext_ref[0:POOL_HALO, :] = jnp.zeros((POOL_HALO, D_GROUP), F32)
        cext_ref[0:CONV_HALO, :] = jnp.zeros((CONV_HALO, D_GROUP), F32)
        kext_ref[0:SWA_WINDOW, :] = jnp.zeros((SWA_WINDOW, SWA_KV_DIM), F32)
        vext_ref[0:SWA_WINDOW, :] = jnp.zeros((SWA_WINDOW, SWA_KV_DIM), F32)

    h_ref[...] = _rmsnorm(xn_ref[...], gpre_ref[...]).astype(BF16)
    proj_chunks = [functools.partial(project_chunk, 1 - cur, c) for c in range(D_IN // PROJ_CHUNK)]

    p_ref = proj_ref.at[cur]

    def put(c0, y, rows=slice(None)):
        mix_ref[rows, c0:c0 + y.shape[1]] = y.astype(BF16)

    def pool_piece():
        u = p_ref[:, OFF_U:OFF_U + D_GROUP]
        for gi, y in enumerate(_pool_project(_pool_delta(uext_ref, u, j * tq), wpool_ref, pscale_ref)):
            put(MIX_POOL + gi * POOL_CH, y)
        npool_ref[...] = uext_ref[tq + POOL_HALO - POOL_PAD:tq + POOL_HALO, :]
        uext_ref[0:POOL_HALO, :] = uext_ref[tq:tq + POOL_HALO, :]

    def conv_piece():
        y = _conv_mixer(cext_ref, p_ref[:, OFF_HC:OFF_HC + D_GROUP], p_ref[:, OFF_GB:OFF_GB + D_GROUP],
                        p_ref[:, OFF_GC:OFF_GC + D_GROUP], convw_ref)
        put(MIX_CONV, y)
        nconv_ref[...] = cext_ref[tq + CONV_HALO - 2:tq + CONV_HALO, :]
        cext_ref[0:CONV_HALO, :] = cext_ref[tq:tq + CONV_HALO, :]

    def kv_fill_piece():
        kext_ref[SWA_WINDOW:SWA_WINDOW + tq, :] = p_ref[:, OFF_K:OFF_K + SWA_KV_DIM]
        vext_ref[SWA_WINDOW:SWA_WINDOW + tq, :] = p_ref[:, OFF_V:OFF_V + SWA_KV_DIM]
        nk_ref[...] = kext_ref[tq:tq + SWA_WINDOW, :]
        nv_ref[...] = vext_ref[tq:tq + SWA_WINDOW, :]

    def swa_piece(sb, h):
        r0 = sb * SWA_WINDOW
        rows = slice(r0, r0 + SWA_WINDOW)
        first_key = jnp.maximum(SWA_WINDOW - (j * tq + r0), 0)
        q_of_pair = lambda pair: p_ref[rows, OFF_Q + pair * LANES:OFF_Q + (pair + 1) * LANES]
        outs = _swa_kv_head(h, q_of_pair, kext_ref[r0:r0 + 2 * SWA_WINDOW, :],
                            vext_ref[r0:r0 + 2 * SWA_WINDOW, :], sinks_ref, l, first_key)
        for pair, y in outs.items():
            put(MIX_SWA + pair * LANES, y, rows)

    def kv_carry_piece():
        kext_ref[0:SWA_WINDOW, :] = kext_ref[tq:tq + SWA_WINDOW, :]
        vext_ref[0:SWA_WINDOW, :] = vext_ref[tq:tq + SWA_WINDOW, :]

    def mem_piece(hm):
        c0 = hm * MEM_HEAD_DIM
        y = _mem_head(p_ref[:, OFF_QM + c0:OFF_QM + c0 + MEM_HEAD_DIM],
                      mk_ref[:, c0:c0 + MEM_HEAD_DIM].astype(BF16), mv_ref[:, c0:c0 + MEM_HEAD_DIM].astype(BF16))
        put(MIX_MEM + c0, y)

    pieces = [pool_piece, conv_piece, kv_fill_piece]
    pieces += [functools.partial(swa_piece, sb, h) for sb in range(tq // SWA_WINDOW) for h in range(2)]
    pieces += [kv_carry_piece] + [functools.partial(mem_piece, hm) for hm in range(MEM_HEADS)]
    for i, piece in enumerate(pieces):
        piece()
        for c in range(i * len(proj_chunks) // len(pieces), (i + 1) * len(proj_chunks) // len(pieces)):
            proj_chunks[c]()

    y = jnp.dot(mix_ref[...], wout_ref[...], preferred_element_type=F32)
    o_ref[...] = xc_ref[...] + _rmsnorm(y, gpost_ref[...])


def _prompt_layer(sinks, x, g_pre, w_in, mk, mv, w_pool, pool_scale, conv_w, w_out, g_post, l, tq, seq):
    m, d = x.shape
    nblk, nsteps, b = seq // tq, m // tq, m // seq
    nmem = mk.shape[1]
    per_seq = lambda r: (r // nblk, 0, 0)
    return pl.pallas_call(
        functools.partial(_prompt_layer_kernel, l=l, tq=tq, nblk=nblk),
        grid=(nsteps,),
        in_specs=[pl.BlockSpec(memory_space=pltpu.SMEM),
                  pl.BlockSpec((tq, d), lambda r: (r, 0)),
                  pl.BlockSpec((tq, d), lambda r: (jnp.minimum(r + 1, nsteps - 1), 0)),
                  _layer_block(g_pre, l), _layer_block(w_in, l),
                  pl.BlockSpec((None, nmem, D_GROUP), per_seq),
                  pl.BlockSpec((None, nmem, D_GROUP), per_seq),
                  _layer_block(w_pool, l), _layer_block(pool_scale, l), _layer_block(conv_w, l),
                  _layer_block(w_out, l), _layer_block(g_post, l)],
        out_specs=[pl.BlockSpec((tq, d), lambda r: (r, 0)),
                   pl.BlockSpec((None, POOL_PAD, D_GROUP), per_seq),
                   pl.BlockSpec((None, CONV_WIDTH - 1, D_GROUP), per_seq),
                   pl.BlockSpec((None, SWA_WINDOW, SWA_KV_DIM), per_seq),
                   pl.BlockSpec((None, SWA_WINDOW, SWA_KV_DIM), per_seq)],
        out_shape=[jax.ShapeDtypeStruct((m, d), F32),
                   jax.ShapeDtypeStruct((b, POOL_PAD, D_GROUP), F32),
                   jax.ShapeDtypeStruct((b, CONV_WIDTH - 1, D_GROUP), F32),
                   jax.ShapeDtypeStruct((b, SWA_WINDOW, SWA_KV_DIM), F32),
                   jax.ShapeDtypeStruct((b, SWA_WINDOW, SWA_KV_DIM), F32)],
        scratch_shapes=[pltpu.VMEM((2, tq, D_IN), F32),
                        pltpu.VMEM((tq, d), BF16),
                        pltpu.VMEM((tq, d), BF16),
                        pltpu.VMEM((POOL_HALO + tq, D_GROUP), F32),
                        pltpu.VMEM((CONV_HALO + tq, D_GROUP), F32),
                        pltpu.VMEM((SWA_WINDOW + tq, SWA_KV_DIM), F32),
                        pltpu.VMEM((SWA_WINDOW + tq, SWA_KV_DIM), F32)],
        compiler_params=_params("arbitrary"),
        name="prompt_layer",
    )(sinks, x, x, g_pre, w_in, mk, mv, w_pool, pool_scale, conv_w, w_out, g_post)


def _sample_mixer_kernel(sinks_ref, proj_ref, pool_ref, conv_ref, kc_ref, vc_ref, mk_ref, mv_ref,
                         wpool_ref, pscale_ref, convw_ref,
                         mix_ref, npool_ref, nconv_ref, nk_ref, nv_ref,
                         uext_all, cext_all, kext_all, vext_all, qs_all, qm_all,
                         *, l, nb, t_new, pos0):
    uext_all[...] = jnp.zeros(uext_all.shape, F32)
    cext_all[...] = jnp.zeros(cext_all.shape, F32)
    kext_all[...] = jnp.zeros(kext_all.shape, F32)
    vext_all[...] = jnp.zeros(vext_all.shape, F32)

    n_swa = SWA_Q_HEADS * t_new
    n_mem = MEM_HEADS * t_new
    low = lax.broadcasted_iota(jnp.int32, (t_new, LANES), 1) < SWA_HEAD_DIM
    rows_of = lambda a: slice(a * t_new, (a + 1) * t_new)

    deltas = []
    for b in range(nb):
        r = rows_of(b)
        uext_ref, cext_ref, kext_ref, vext_ref = uext_all.at[b], cext_all.at[b], kext_all.at[b], vext_all.at[b]
        uext_ref[POOL_HALO - POOL_PAD:POOL_HALO, :] = pool_ref[b]
        deltas.append(_pool_delta(uext_ref, proj_ref[r, OFF_U:OFF_U + D_GROUP], pos0))
        npool_ref[b] = uext_ref[t_new + POOL_HALO - POOL_PAD:t_new + POOL_HALO, :]

        cext_ref[CONV_HALO - 2:CONV_HALO, :] = conv_ref[b]
        mix_ref[r, MIX_CONV:MIX_CONV + D_GROUP] = _conv_mixer(
            cext_ref, proj_ref[r, OFF_HC:OFF_HC + D_GROUP], proj_ref[r, OFF_GB:OFF_GB + D_GROUP],
            proj_ref[r, OFF_GC:OFF_GC + D_GROUP], convw_ref)
        nconv_ref[b] = cext_ref[t_new + CONV_HALO - 2:t_new + CONV_HALO, :]

        kext_ref[0:SWA_WINDOW, :] = kc_ref[b]
        vext_ref[0:SWA_WINDOW, :] = vc_ref[b]
        kext_ref[SWA_WINDOW:SWA_WINDOW + t_new, :] = proj_ref[r, OFF_K:OFF_K + SWA_KV_DIM]
        vext_ref[SWA_WINDOW:SWA_WINDOW + t_new, :] = proj_ref[r, OFF_V:OFF_V + SWA_KV_DIM]
        nk_ref[b] = kext_ref[t_new:t_new + SWA_WINDOW, :]
        nv_ref[b] = vext_ref[t_new:t_new + SWA_WINDOW, :]
        for pair in range(SWA_Q_HEADS // 2):
            qp = proj_ref[r, OFF_Q + pair * LANES:OFF_Q + (pair + 1) * LANES] * SWA_SCALE
            qr = pltpu.roll(qp, SWA_HEAD_DIM, 1)
            if pair // 2 == 0:
                q_even, q_odd = jnp.where(low, qp, 0.0), jnp.where(low, qr, 0.0)
            else:
                q_even, q_odd = jnp.where(low, 0.0, qr), jnp.where(low, 0.0, qp)
            qs_all[b, rows_of(2 * pair), :] = q_even
            qs_all[b, rows_of(2 * pair + 1), :] = q_odd
        for hm in range(MEM_HEADS):
            qm_all[b, rows_of(hm), :] = (
                proj_ref[r, OFF_QM + hm * MEM_HEAD_DIM:OFF_QM + (hm + 1) * MEM_HEAD_DIM])

    deltas = [jnp.concatenate([d[gi] for d in deltas], axis=0) for gi in range(len(POOL_WINDOWS))]
    for gi, y in enumerate(_pool_project(deltas, wpool_ref, pscale_ref)):
        mix_ref[:, MIX_POOL + gi * POOL_CH:MIX_POOL + (gi + 1) * POOL_CH] = y

    nt_dims = (((1,), (1,)), ((), ()))
    s_swa = jnp.concatenate(
        [lax.dot_general(qs_all[b].astype(BF16), kext_all[b].astype(BF16),
                         nt_dims, preferred_element_type=F32) for b in range(nb)], axis=0)
    s_mem = jnp.concatenate(
        [lax.dot_general(qm_all[b].astype(BF16), mk_ref[b].astype(BF16),
                         nt_dims, preferred_element_type=F32) for b in range(nb)], axis=0)

    ri = lax.broadcasted_iota(jnp.int32, s_swa.shape, 0)
    tok = ri % t_new
    kj = lax.broadcasted_iota(jnp.int32, s_swa.shape, 1)
    valid = (kj > tok) & (kj <= tok + SWA_WINDOW) & (kj >= max(SWA_WINDOW - pos0, 0))
    head = (lax.broadcasted_iota(jnp.int32, (s_swa.shape[0], 1), 0) // t_new) % SWA_Q_HEADS
    sink = jnp.zeros(head.shape, F32)
    for hq in range(SWA_Q_HEADS):
        sink = jnp.where(head == hq, sinks_ref[l, hq], sink)
    s_swa = jnp.where(valid, s_swa, -jnp.inf)
    m = jnp.maximum(jnp.max(s_swa, axis=1, keepdims=True), sink)
    e = jnp.exp(s_swa - m)
    p_swa = (e * (1.0 / (jnp.sum(e, axis=1, keepdims=True) + jnp.exp(sink - m)))).astype(BF16)

    mem_valid = (lax.broadcasted_iota(jnp.int32, s_mem.shape, 1) % MEM_HEADS
                 == (lax.broadcasted_iota(jnp.int32, s_mem.shape, 0) // t_new) % MEM_HEADS)
    s_mem = jnp.where(mem_valid, s_mem * MEM_SCALE, -jnp.inf)
    e = jnp.exp(s_mem - jnp.max(s_mem, axis=1, keepdims=True))
    p_mem = (e * (1.0 / jnp.sum(e, axis=1, keepdims=True))).astype(BF16)

    for b in range(nb):
        r = rows_of(b)
        o = jnp.dot(p_swa[b * n_swa:(b + 1) * n_swa, :], vext_all[b].astype(BF16), preferred_element_type=F32)
        o_roll = pltpu.roll(o, SWA_HEAD_DIM, 1)
        for pair in range(SWA_Q_HEADS // 2):
            src_even, src_odd = (o, o_roll) if pair // 2 == 0 else (o_roll, o)
            mix_ref[r, MIX_SWA + pair * LANES:MIX_SWA + (pair + 1) * LANES] = jnp.where(
                low, src_even[rows_of(2 * pair), :], src_odd[rows_of(2 * pair + 1), :])
        o = jnp.dot(p_mem[b * n_mem:(b + 1) * n_mem, :], mv_ref[b].astype(BF16), preferred_element_type=F32)
        for hm in range(MEM_HEADS):
            mix_ref[r, MIX_MEM + hm * MEM_HEAD_DIM:MIX_MEM + (hm + 1) * MEM_HEAD_DIM] = o[rows_of(hm), :]


def _sample_mixer(sinks, proj, pool, conv, kc, vc, mk, mv, w_pool, pool_scale, conv_w, l, nb, t_new, pos0):
    b = proj.shape[0] // t_new
    slots = nb
    blk = lambda a: pl.BlockSpec((None, nb) + a.shape[2:], lambda i: (l, i) + (0,) * (a.ndim - 2))
    out_blk = lambda a: pl.BlockSpec((nb,) + a.shape[2:], lambda i: (i,) + (0,) * (a.ndim - 2))
    out_sds = lambda a: jax.ShapeDtypeStruct(a.shape[1:], F32)
    return pl.pallas_call(
        functools.partial(_sample_mixer_kernel, l=l, nb=nb, t_new=t_new, pos0=pos0),
        grid=(b // nb,),
        in_specs=[pl.BlockSpec(memory_space=pltpu.SMEM),
                  pl.BlockSpec((nb * t_new, D_IN), lambda i: (i, 0)),
                  blk(pool), blk(conv), blk(kc), blk(vc), blk(mk), blk(mv),
                  _layer_block(w_pool, l), _layer_block(pool_scale, l), _layer_block(conv_w, l)],
        out_specs=[pl.BlockSpec((nb * t_new, 4 * D_GROUP), lambda i: (i, 0)),
                   out_blk(pool), out_blk(conv), out_blk(kc), out_blk(vc)],
        out_shape=[jax.ShapeDtypeStruct((b * t_new, 4 * D_GROUP), F32),
                   out_sds(pool), out_sds(conv), out_sds(kc), out_sds(vc)],
        scratch_shapes=[pltpu.VMEM((slots, POOL_HALO + 8, D_GROUP), F32),
                        pltpu.VMEM((slots, CONV_HALO + 8, D_GROUP), F32),
                        pltpu.VMEM((slots, 2 * SWA_WINDOW, SWA_KV_DIM), F32),
                        pltpu.VMEM((slots, 2 * SWA_WINDOW, SWA_KV_DIM), F32),
                        pltpu.VMEM((slots, SWA_Q_HEADS * t_new, LANES), F32),
                        pltpu.VMEM((slots, MEM_HEADS * t_new, MEM_HEAD_DIM), F32)],
        compiler_params=_params("arbitrary"),
        name="sample_mixer",
    )(sinks, proj, pool, conv, kc, vc, mk, mv, w_pool, pool_scale, conv_w)


def kernel(x_prompt, x_sample, mem_prompt, state_pool, state_conv, cache_swa_k, cache_swa_v,
           cache_mem_k, cache_mem_v, g_mix_pre, w_in, w_pool, pool_scale, conv_w, swa_sinks,
           g_mem, w_mem_kv, w_out, g_mix_post, g_mlp_pre, w_up, w_down, g_mlp_post):
    depth = w_in.shape[0]
    bp, seq, d_model = x_prompt.shape
    bs, t_new, _ = x_sample.shape
    nmem = mem_prompt.shape[1]
    assert w_in.shape[2] == D_IN and d_model == 4 * D_GROUP

    w_in_b, w_mem_b, w_out_b = w_in.astype(BF16), w_mem_kv.astype(BF16), w_out.astype(BF16)
    w_pool_b = w_pool.astype(BF16)
    rows = lambda a: a.reshape(depth, 1, a.shape[-1])
    g_mix_pre, g_mem, g_mix_post = rows(g_mix_pre), rows(g_mem), rows(g_mix_post)
    g_mlp_pre, g_mlp_post, pool_scale = rows(g_mlp_pre), rows(g_mlp_post), rows(pool_scale)

    yp = x_prompt.reshape(bp * seq, d_model)
    ys = x_sample.reshape(bs * t_new, d_model)
    mem = mem_prompt.reshape(bp * nmem, d_model)
    kc = cache_swa_k.reshape(depth, bs, SWA_WINDOW, SWA_KV_DIM)
    vc = cache_swa_v.reshape(depth, bs, SWA_WINDOW, SWA_KV_DIM)
    mkc = cache_mem_k.reshape(depth, bs, nmem * MEM_HEADS, MEM_HEAD_DIM)
    mvc = cache_mem_v.reshape(depth, bs, nmem * MEM_HEADS, MEM_HEAD_DIM)

    outs = [[] for _ in range(10)]
    for l in range(depth):
        proj = _norm_matmul(ys, g_mix_pre, w_in_b, l, tm=512)
        mix, pool_s, conv_s, k_s, v_s = _sample_mixer(
            swa_sinks, proj, state_pool, state_conv, kc, vc, mkc, mvc, w_pool_b, pool_scale, conv_w,
            l, nb=8, t_new=t_new, pos0=PAST_LEN)
        ys = _matmul_norm_res(mix, w_out_b, g_mix_post, ys, l, tm=512)
        ys, w_up_b, w_down_b = _mlp(ys, g_mlp_pre, w_up, w_down, g_mlp_post, l, tm=bs * t_new, tf=512)

        mk, mv = _norm_matmul(mem, g_mem, w_mem_b, l, tm=512, n_out=2)
        mk, mv = mk.reshape(bp, nmem, D_GROUP), mv.reshape(bp, nmem, D_GROUP)
        yp, pool_p, conv_p, k_p, v_p = _prompt_layer(swa_sinks, yp, g_mix_pre, w_in_b, mk, mv, w_pool_b, pool_scale,
                                                     conv_w, w_out_b, g_mix_post, l, tq=256, seq=seq)
        yp = _mlp(yp, g_mlp_pre, w_up_b, w_down_b, g_mlp_post, l, tm=1024, tf=512)

        kv_shape = lambda a: a.reshape(a.shape[0], SWA_WINDOW, 2, SWA_HEAD_DIM)
        mem_shape = lambda a: a.reshape(bp, nmem, MEM_HEADS, MEM_HEAD_DIM)
        for lst, val in zip(outs, (pool_p, pool_s, conv_p, conv_s, kv_shape(k_p), kv_shape(k_s),
                                   kv_shape(v_p), kv_shape(v_s), mem_shape(mk), mem_shape(mv))):
            lst.append(val)

    return (yp.reshape(bp, seq, d_model), ys.reshape(bs, t_new, d_model)) + tuple(jnp.stack(o) for o in outs)
```

```python
import functools
import math

import jax
import jax.numpy as jnp
from jax import lax
from jax.experimental import pallas as pl
from jax.experimental.pallas import tpu as pltpu

F32 = jnp.float32
BF16 = jnp.bfloat16

LANES = 128
D_GROUP = 512
POOL_WINDOWS = (2, 4, 8, 16)
POOL_CH = D_GROUP // len(POOL_WINDOWS)
POOL_PAD = max(POOL_WINDOWS) - 1
POOL_HALO = 16
CONV_WIDTH = 3
CONV_HALO = 8
SWA_WINDOW = 128
SWA_HEAD_DIM = 64
SWA_Q_HEADS = 8
SWA_GROUP = 4
SWA_KV_DIM = 128
SWA_SCALE = 1.0 / math.sqrt(SWA_HEAD_DIM)
MEM_HEADS = 4
MEM_HEAD_DIM = D_GROUP // MEM_HEADS
MEM_SCALE = 1.0 / math.sqrt(MEM_HEAD_DIM)
RMS_EPS = 1e-6
PAST_LEN = 8192
OFF_U, OFF_HC, OFF_GB, OFF_GC, OFF_Q = 0, 512, 1024, 1536, 2048
OFF_K, OFF_V, OFF_QM, D_IN = 2560, 2688, 2816, 3328
MIX_POOL, MIX_CONV, MIX_SWA, MIX_MEM = 0, 512, 1024, 1536

VMEM_LIMIT_BYTES = 56 * 1024 * 1024


def _rmsnorm(x, g):
    ms = jnp.mean(x * x, axis=-1, keepdims=True)
    return x * lax.rsqrt(ms + RMS_EPS) * g


def _params(*sem, flags=None):
    return pltpu.CompilerParams(dimension_semantics=sem, vmem_limit_bytes=VMEM_LIMIT_BYTES, flags=flags)


def _layer_block(a, l):
    rest = (0,) * (a.ndim - 1)
    return pl.BlockSpec((None,) + a.shape[1:], lambda *_: (l,) + rest, pipeline_mode=pl.Buffered(1))


def _norm_matmul_kernel(x_ref, g_ref, w_ref, *o_refs):
    h = _rmsnorm(x_ref[...], g_ref[...]).astype(BF16)
    y = jnp.dot(h, w_ref[...], preferred_element_type=F32)
    n = y.shape[1] // len(o_refs)
    for i, o_ref in enumerate(o_refs):
        o_ref[...] = y[:, i * n:(i + 1) * n]


def _norm_matmul(x, g, w, l, tm, n_out=1):
    m, k = x.shape
    n = w.shape[2] // n_out
    outs = pl.pallas_call(
        _norm_matmul_kernel,
        grid=(m // tm,),
        in_specs=[pl.BlockSpec((tm, k), lambda i: (i, 0)), _layer_block(g, l), _layer_block(w, l)],
        out_specs=[pl.BlockSpec((tm, n), lambda i: (i, 0))] * n_out,
        out_shape=[jax.ShapeDtypeStruct((m, n), F32)] * n_out,
        compiler_params=_params("arbitrary"),
        name="norm_matmul",
    )(x, g, w)
    return outs[0] if n_out == 1 else outs


def _matmul_norm_res_kernel(a_ref, w_ref, g_ref, x_ref, o_ref):
    y = jnp.dot(a_ref[...].astype(BF16), w_ref[...], preferred_element_type=F32)
    o_ref[...] = x_ref[...] + _rmsnorm(y, g_ref[...])


def _matmul_norm_res(a, w, g, x, l, tm):
    m, k = a.shape
    n = w.shape[2]
    return pl.pallas_call(
        _matmul_norm_res_kernel,
        grid=(m // tm,),
        in_specs=[pl.BlockSpec((tm, k), lambda i: (i, 0)), _layer_block(w, l), _layer_block(g, l),
                  pl.BlockSpec((tm, n), lambda i: (i, 0))],
        out_specs=pl.BlockSpec((tm, n), lambda i: (i, 0)),
        out_shape=jax.ShapeDtypeStruct((m, n), F32),
        compiler_params=_params("arbitrary"),
        name="matmul_norm_res",
    )(a, w, g, x)


def _mlp_kernel(x_ref, gpre_ref, wup_ref, wdown_ref, gpost_ref, o_ref, *rest, cast_weights):
    f = pl.program_id(1)
    h_ref = rest[-1]

    @pl.when(f == 0)
    def _():
        h_ref[...] = _rmsnorm(x_ref[...], gpre_ref[...]).astype(BF16)
        o_ref[...] = jnp.zeros(o_ref.shape, F32)

    if cast_weights:
        wup_b_ref, wdown_b_ref = rest[:2]
        wup, wdown = wup_ref[...].astype(BF16), wdown_ref[...].astype(BF16)
        wup_b_ref[...] = wup
        wdown_b_ref[...] = wdown
    else:
        wup, wdown = wup_ref[...], wdown_ref[...]
    a = jnp.dot(h_ref[...], wup, preferred_element_type=F32)
    a = jnp.square(jnp.maximum(a, 0.0)).astype(BF16)
    o_ref[...] += jnp.dot(a, wdown, preferred_element_type=F32)

    @pl.when(f == pl.num_programs(1) - 1)
    def _():
        o_ref[...] = x_ref[...] + _rmsnorm(o_ref[...], gpost_ref[...])


def _mlp(x, g_pre, w_up, w_down, g_post, l, tm, tf):
    m, d = x.shape
    cast_weights = w_up.ndim == 3
    ff = w_up.shape[-1]
    if cast_weights:
        assert m == tm, "each weight block must be visited exactly once"
        w_specs = [pl.BlockSpec((None, d, tf), lambda i, f: (l, 0, f)),
                   pl.BlockSpec((None, tf, d), lambda i, f: (l, f, 0))]
    else:
        w_specs = [pl.BlockSpec((d, tf), lambda i, f: (0, f)), pl.BlockSpec((tf, d), lambda i, f: (f, 0))]
    out_specs = [pl.BlockSpec((tm, d), lambda i, f: (i, 0))]
    out_shape = [jax.ShapeDtypeStruct((m, d), F32)]
    if cast_weights:
        out_specs += [pl.BlockSpec((d, tf), lambda i, f: (0, f)), pl.BlockSpec((tf, d), lambda i, f: (f, 0))]
        out_shape += [jax.ShapeDtypeStruct((d, ff), BF16), jax.ShapeDtypeStruct((ff, d), BF16)]
    outs = pl.pallas_call(
        functools.partial(_mlp_kernel, cast_weights=cast_weights),
        grid=(m // tm, ff // tf),
        in_specs=[pl.BlockSpec((tm, d), lambda i, f: (i, 0)), _layer_block(g_pre, l)] + w_specs
                 + [_layer_block(g_post, l)],
        out_specs=out_specs,
        out_shape=out_shape,
        scratch_shapes=[pltpu.VMEM((tm, d), BF16)],
        compiler_params=_params("arbitrary", "arbitrary"),
        name="mlp",
    )(x, g_pre, w_up, w_down, g_post)
    return outs if cast_weights else outs[0]


def _pool_delta(uext_ref, u, pos0):
    rows = u.shape[0]
    uext_ref[POOL_HALO:POOL_HALO + rows, :] = u
    pos = pos0 + lax.broadcasted_iota(jnp.int32, (rows, POOL_CH), 0)
    outs = []
    for gi, w in enumerate(POOL_WINDOWS):
        c0 = gi * POOL_CH
        s = u[:, c0:c0 + POOL_CH]
        for back in range(1, w):
            s = s + uext_ref[POOL_HALO - back:POOL_HALO - back + rows, c0:c0 + POOL_CH]
        cnt = jnp.minimum(pos + 1, w).astype(F32)
        outs.append(s / cnt - u[:, c0:c0 + POOL_CH])
    return outs


def _pool_project(deltas, wpool_ref, pscale_ref):
    outs = []
    for gi, d in enumerate(deltas):
        y = jnp.dot(d.astype(BF16), wpool_ref[gi], preferred_element_type=F32)
        outs.append(y * pscale_ref[:, gi * POOL_CH:(gi + 1) * POOL_CH])
    return outs


def _conv_mixer(cext_ref, hc, gb, gc, convw_ref):
    rows = hc.shape[0]
    cext_ref[CONV_HALO:CONV_HALO + rows, :] = gc * hc
    conv = convw_ref[0:1, :] * cext_ref[CONV_HALO - 2:CONV_HALO - 2 + rows, :]
    for kk in range(1, CONV_WIDTH):
        lo = CONV_HALO - 2 + kk
        conv = conv + convw_ref[kk:kk + 1, :] * cext_ref[lo:lo + rows, :]
    return gb * conv


def _swa_kv_head(h, q_of_pair, kwin, vwin, sinks_ref, l, first_key):
    nk = kwin.shape[0]
    low = lax.broadcasted_iota(jnp.int32, (nk, LANES), 1) < SWA_HEAD_DIM
    kroll = pltpu.roll(kwin, SWA_HEAD_DIM, 1)
    vroll = pltpu.roll(vwin, SWA_HEAD_DIM, 1)
    ind_a = jnp.where(low, 1.0, 0.0)
    ind_b = 1.0 - ind_a
    if h == 0:
        k_a, k_b = jnp.where(low, kwin, 0.0), jnp.where(low, 0.0, kroll)
        v_a, v_b = jnp.where(low, vwin, 0.0), jnp.where(low, 0.0, vroll)
    else:
        k_a, k_b = jnp.where(low, kroll, 0.0), jnp.where(low, 0.0, kwin)
        v_a, v_b = jnp.where(low, vroll, 0.0), jnp.where(low, 0.0, vwin)
    kst = jnp.concatenate([k_a, k_b], axis=0).astype(BF16)
    vst = jnp.concatenate([jnp.concatenate([v_a, ind_a], axis=1),
                           jnp.concatenate([v_b, ind_b], axis=1)], axis=0).astype(BF16)
    outs = {}
    for r in range(2):
        pair = 2 * h + r
        q = q_of_pair(pair)
        rows = q.shape[0]
        qi = lax.broadcasted_iota(jnp.int32, (rows, 2 * nk), 0)
        kj = lax.broadcasted_iota(jnp.int32, (rows, 2 * nk), 1) & (nk - 1)
        valid = (kj > qi) & (kj <= qi + SWA_WINDOW) & (kj >= first_key)
        s = lax.dot_general((q * SWA_SCALE).astype(BF16), kst, (((1,), (1,)), ((), ())),
                            preferred_element_type=F32)
        s = jnp.where(valid, s, -jnp.inf)
        sink_a = sinks_ref[l, 4 * h + 2 * r]
        sink_b = sinks_ref[l, 4 * h + 2 * r + 1]
        m_a = jnp.maximum(jnp.max(s[:, :nk], axis=1, keepdims=True), sink_a)
        m_b = jnp.maximum(jnp.max(s[:, nk:], axis=1, keepdims=True), sink_b)
        p = jnp.concatenate([jnp.exp(s[:, :nk] - m_a), jnp.exp(s[:, nk:] - m_b)],
                            axis=1).astype(BF16)
        o = jnp.dot(p, vst, preferred_element_type=F32)
        low_q = lax.broadcasted_iota(jnp.int32, (rows, LANES), 1) < SWA_HEAD_DIM
        den = o[:, LANES:] + jnp.where(low_q, jnp.exp(sink_a - m_a), jnp.exp(sink_b - m_b))
        outs[pair] = o[:, :LANES] / den
    return outs


def _mem_head(qm_h, mk_h, mv_h):
    s = lax.dot_general(qm_h.astype(BF16), mk_h, (((1,), (1,)), ((), ())),
                        preferred_element_type=F32) * MEM_SCALE
    p = jnp.exp(s - jnp.max(s, axis=1, keepdims=True)).astype(BF16)
    vext = jnp.concatenate([mv_h, jnp.ones(mv_h.shape, BF16)], axis=1)
    o = jnp.dot(p, vext, preferred_element_type=F32)
    return o[:, :MEM_HEAD_DIM] / o[:, MEM_HEAD_DIM:]


PROJ_CHUNK = 512
PROJ_LEAD = 2
N_PROJ_CHUNKS = -(-D_IN // PROJ_CHUNK)


def _prompt_layer_kernel(sinks_ref, xc_ref, x1_ref, xn_ref, gpre_ref, win_ref, mk_ref, mv_ref, wpool_ref, pscale_ref,
                         convw_ref, wout_ref, gpost_ref,
                         o_ref, npool_ref, nconv_ref, nk_ref, nv_ref,
                         proja_ref, projb_ref, h_ref, mix_ref, uext_ref, cext_ref, kext_ref, vext_ref,
                         *, l, tq, nblk):
    r = pl.program_id(0)
    j = r % nblk

    def project_chunk(dst_ref, c):
        cols = slice(c * PROJ_CHUNK, min((c + 1) * PROJ_CHUNK, D_IN))
        dst_ref[:, cols] = jnp.dot(h_ref[...], win_ref[:, cols], preferred_element_type=F32)

    @pl.when(r == 0)
    def _():
        h_ref[...] = _rmsnorm(xc_ref[...], gpre_ref[...]).astype(BF16)
        for c in range(N_PROJ_CHUNKS):
            project_chunk(proja_ref, c)
        h_ref[...] = _rmsnorm(x1_ref[...], gpre_ref[...]).astype(BF16)

    @pl.when(j == 0)
    def _():
        uext_ref[0:POOL_HALO, :] = jnp.zeros((POOL_HALO, D_GROUP), F32)
        cext_ref[0:CONV_HALO, :] = jnp.zeros((CONV_HALO, D_GROUP), F32)
        kext_ref[0:SWA_WINDOW, :] = jnp.zeros((SWA_WINDOW, SWA_KV_DIM), F32)
        vext_ref[0:SWA_WINDOW, :] = jnp.zeros((SWA_WINDOW, SWA_KV_DIM), F32)

    def put(c0, y, rows=slice(None)):
        mix_ref[rows, c0:c0 + y.shape[1]] = y.astype(BF16)

    def mix_block(p_ref, pn_ref):
        proj_chunks = [functools.partial(project_chunk, pn_ref, c) for c in range(N_PROJ_CHUNKS)]

        def pool_piece():
            u = p_ref[:, OFF_U:OFF_U + D_GROUP]
            for gi, y in enumerate(_pool_project(_pool_delta(uext_ref, u, j * tq), wpool_ref, pscale_ref)):
                put(MIX_POOL + gi * POOL_CH, y)
            npool_ref[...] = uext_ref[tq + POOL_HALO - POOL_PAD:tq + POOL_HALO, :]
            uext_ref[0:POOL_HALO, :] = uext_ref[tq:tq + POOL_HALO, :]

        def conv_piece():
            y = _conv_mixer(cext_ref, p_ref[:, OFF_HC:OFF_HC + D_GROUP], p_ref[:, OFF_GB:OFF_GB + D_GROUP],
                            p_ref[:, OFF_GC:OFF_GC + D_GROUP], convw_ref)
            put(MIX_CONV, y)
            nconv_ref[...] = cext_ref[tq + CONV_HALO - 2:tq + CONV_HALO, :]
            cext_ref[0:CONV_HALO, :] = cext_ref[tq:tq + CONV_HALO, :]

        def kv_fill_piece():
            kext_ref[SWA_WINDOW:SWA_WINDOW + tq, :] = p_ref[:, OFF_K:OFF_K + SWA_KV_DIM]
            vext_ref[SWA_WINDOW:SWA_WINDOW + tq, :] = p_ref[:, OFF_V:OFF_V + SWA_KV_DIM]
            nk_ref[...] = kext_ref[tq:tq + SWA_WINDOW, :]
            nv_ref[...] = vext_ref[tq:tq + SWA_WINDOW, :]

        def swa_piece(sb, h):
            r0 = sb * SWA_WINDOW
            rows = slice(r0, r0 + SWA_WINDOW)
            first_key = jnp.maximum(SWA_WINDOW - (j * tq + r0), 0)
            q_of_pair = lambda pair: p_ref[rows, OFF_Q + pair * LANES:OFF_Q + (pair + 1) * LANES]
            outs = _swa_kv_head(h, q_of_pair, kext_ref[r0:r0 + 2 * SWA_WINDOW, :],
                                vext_ref[r0:r0 + 2 * SWA_WINDOW, :], sinks_ref, l, first_key)
            for pair, y in outs.items():
                put(MIX_SWA + pair * LANES, y, rows)

        def kv_carry_piece():
            kext_ref[0:SWA_WINDOW, :] = kext_ref[tq:tq + SWA_WINDOW, :]
            vext_ref[0:SWA_WINDOW, :] = vext_ref[tq:tq + SWA_WINDOW, :]

        def mem_piece(hm):
            c0 = hm * MEM_HEAD_DIM
            y = _mem_head(p_ref[:, OFF_QM + c0:OFF_QM + c0 + MEM_HEAD_DIM],
                          mk_ref[:, c0:c0 + MEM_HEAD_DIM].astype(BF16), mv_ref[:, c0:c0 + MEM_HEAD_DIM].astype(BF16))
            put(MIX_MEM + c0, y)

        pieces = [pool_piece, conv_piece, kv_fill_piece]
        pieces += [functools.partial(swa_piece, sb, h) for sb in range(tq // SWA_WINDOW) for h in range(2)]
        pieces += [kv_carry_piece] + [functools.partial(mem_piece, hm) for hm in range(MEM_HEADS)]
        for c in range(PROJ_LEAD):
            proj_chunks[c]()
        rest = proj_chunks[PROJ_LEAD:]
        for i, piece in enumerate(pieces):
            piece()
            for c in range(i * len(rest) // len(pieces), (i + 1) * len(rest) // len(pieces)):
                rest[c]()

        half = tq // 2
        y0 = jnp.dot(mix_ref[0:half, :], wout_ref[...], preferred_element_type=F32)
        h_ref[...] = _rmsnorm(xn_ref[...], gpre_ref[...]).astype(BF16)
        y1 = jnp.dot(mix_ref[half:tq, :], wout_ref[...], preferred_element_type=F32)
        o_ref[0:half, :] = xc_ref[0:half, :] + _rmsnorm(y0, gpost_ref[...])
        o_ref[half:tq, :] = xc_ref[half:tq, :] + _rmsnorm(y1, gpost_ref[...])

    @pl.when(r % 2 == 0)
    def _():
        mix_block(proja_ref, projb_ref)

    @pl.when(r % 2 == 1)
    def _():
        mix_block(projb_ref, proja_ref)


def _prompt_layer(sinks, x, g_pre, w_in, mk, mv, w_pool, pool_scale, conv_w, w_out, g_post, l, tq, seq):
    m, d = x.shape
    nblk, nsteps, b = seq // tq, m // tq, m // seq
    nmem = mk.shape[1]
    per_seq = lambda r: (r // nblk, 0, 0)
    return pl.pallas_call(
        functools.partial(_prompt_layer_kernel, l=l, tq=tq, nblk=nblk),
        grid=(nsteps,),
        in_specs=[pl.BlockSpec(memory_space=pltpu.SMEM),
                  pl.BlockSpec((tq, d), lambda r: (r, 0)),
                  pl.BlockSpec((tq, d), lambda r: (1, 0), pipeline_mode=pl.Buffered(1)),
                  pl.BlockSpec((tq, d), lambda r: (jnp.minimum(r + 2, nsteps - 1), 0)),
                  _layer_block(g_pre, l), _layer_block(w_in, l),
                  pl.BlockSpec((None, nmem, D_GROUP), per_seq),
                  pl.BlockSpec((None, nmem, D_GROUP), per_seq),
                  _layer_block(w_pool, l), _layer_block(pool_scale, l), _layer_block(conv_w, l),
                  _layer_block(w_out, l), _layer_block(g_post, l)],
        out_specs=[pl.BlockSpec((tq, d), lambda r: (r, 0)),
                   pl.BlockSpec((None, POOL_PAD, D_GROUP), per_seq),
                   pl.BlockSpec((None, CONV_WIDTH - 1, D_GROUP), per_seq),
                   pl.BlockSpec((None, SWA_WINDOW, SWA_KV_DIM), per_seq),
                   pl.BlockSpec((None, SWA_WINDOW, SWA_KV_DIM), per_seq)],
        out_shape=[jax.ShapeDtypeStruct((m, d), F32),
                   jax.ShapeDtypeStruct((b, POOL_PAD, D_GROUP), F32),
                   jax.ShapeDtypeStruct((b, CONV_WIDTH - 1, D_GROUP), F32),
                   jax.ShapeDtypeStruct((b, SWA_WINDOW, SWA_KV_DIM), F32),
                   jax.ShapeDtypeStruct((b, SWA_WINDOW, SWA_KV_DIM), F32)],
        scratch_shapes=[pltpu.VMEM((tq, D_IN), F32),
                        pltpu.VMEM((tq, D_IN), F32),
                        pltpu.VMEM((tq, d), BF16),
                        pltpu.VMEM((tq, d), BF16),
                        pltpu.VMEM((POOL_HALO + tq, D_GROUP), F32),
                        pltpu.VMEM((CONV_HALO + tq, D_GROUP), F32),
                        pltpu.VMEM((SWA_WINDOW + tq, SWA_KV_DIM), F32),
                        pltpu.VMEM((SWA_WINDOW + tq, SWA_KV_DIM), F32)],
        compiler_params=_params("arbitrary"),
        name="prompt_layer",
    )(sinks, x, x, x, g_pre, w_in, mk, mv, w_pool, pool_scale, conv_w, w_out, g_post)


def _sample_mixer_kernel(sinks_ref, proj_ref, pool_ref, conv_ref, kc_ref, vc_ref, mk_ref, mv_ref,
                         wpool_ref, pscale_ref, convw_ref, *refs, l, nb, t_new, pos0):
    mix_ref, npool_ref, nconv_ref, nk_ref, nv_ref = refs[-11:-6]
    uext_all, cext_all, kext_all, vext_all, qs_all, qm_all = refs[-6:]
    uext_all[...] = jnp.zeros(uext_all.shape, F32)
    cext_all[...] = jnp.zeros(cext_all.shape, F32)
    kext_all[...] = jnp.zeros(kext_all.shape, F32)
    vext_all[...] = jnp.zeros(vext_all.shape, F32)

    n_swa = SWA_Q_HEADS * t_new
    n_mem = MEM_HEADS * t_new
    low = lax.broadcasted_iota(jnp.int32, (t_new, LANES), 1) < SWA_HEAD_DIM
    rows_of = lambda a: slice(a * t_new, (a + 1) * t_new)

    deltas = []
    for b in range(nb):
        r = rows_of(b)
        uext_ref, cext_ref, kext_ref, vext_ref = uext_all.at[b], cext_all.at[b], kext_all.at[b], vext_all.at[b]
        uext_ref[POOL_HALO - POOL_PAD:POOL_HALO, :] = pool_ref[b]
        deltas.append(_pool_delta(uext_ref, proj_ref[r, OFF_U:OFF_U + D_GROUP], pos0))
        npool_ref[b] = uext_ref[t_new + POOL_HALO - POOL_PAD:t_new + POOL_HALO, :]

        cext_ref[CONV_HALO - 2:CONV_HALO, :] = conv_ref[b]
        mix_ref[r, MIX_CONV:MIX_CONV + D_GROUP] = _conv_mixer(
            cext_ref, proj_ref[r, OFF_HC:OFF_HC + D_GROUP], proj_ref[r, OFF_GB:OFF_GB + D_GROUP],
            proj_ref[r, OFF_GC:OFF_GC + D_GROUP], convw_ref)
        nconv_ref[b] = cext_ref[t_new + CONV_HALO - 2:t_new + CONV_HALO, :]

        kext_ref[0:SWA_WINDOW, :] = kc_ref[b]
        vext_ref[0:SWA_WINDOW, :] = vc_ref[b]
        kext_ref[SWA_WINDOW:SWA_WINDOW + t_new, :] = proj_ref[r, OFF_K:OFF_K + SWA_KV_DIM]
        vext_ref[SWA_WINDOW:SWA_WINDOW + t_new, :] = proj_ref[r, OFF_V:OFF_V + SWA_KV_DIM]
        nk_ref[b] = kext_ref[t_new:t_new + SWA_WINDOW, :]
        nv_ref[b] = vext_ref[t_new:t_new + SWA_WINDOW, :]
        for pair in range(SWA_Q_HEADS // 2):
            qp = proj_ref[r, OFF_Q + pair * LANES:OFF_Q + (pair + 1) * LANES] * SWA_SCALE
            qr = pltpu.roll(qp, SWA_HEAD_DIM, 1)
            if pair // 2 == 0:
                q_even, q_odd = jnp.where(low, qp, 0.0), jnp.where(low, qr, 0.0)
            else:
                q_even, q_odd = jnp.where(low, 0.0, qr), jnp.where(low, 0.0, qp)
            qs_all[b, rows_of(2 * pair), :] = q_even
            qs_all[b, rows_of(2 * pair + 1), :] = q_odd
        for hm in range(MEM_HEADS):
            qm_all[b, rows_of(hm), :] = (
                proj_ref[r, OFF_QM + hm * MEM_HEAD_DIM:OFF_QM + (hm + 1) * MEM_HEAD_DIM])

    deltas = [jnp.concatenate([d[gi] for d in deltas], axis=0) for gi in range(len(POOL_WINDOWS))]
    for gi, y in enumerate(_pool_project(deltas, wpool_ref, pscale_ref)):
        mix_ref[:, MIX_POOL + gi * POOL_CH:MIX_POOL + (gi + 1) * POOL_CH] = y

    nt_dims = (((1,), (1,)), ((), ()))
    s_swa = jnp.concatenate(
        [lax.dot_general(qs_all[b].astype(BF16), kext_all[b].astype(BF16),
                         nt_dims, preferred_element_type=F32) for b in range(nb)], axis=0)
    s_mem = jnp.concatenate(
        [lax.dot_general(qm_all[b].astype(BF16), mk_ref[b].astype(BF16),
                         nt_dims, preferred_element_type=F32) for b in range(nb)], axis=0)

    ri = lax.broadcasted_iota(jnp.int32, s_swa.shape, 0)
    tok = ri % t_new
    kj = lax.broadcasted_iota(jnp.int32, s_swa.shape, 1)
    valid = (kj > tok) & (kj <= tok + SWA_WINDOW) & (kj >= max(SWA_WINDOW - pos0, 0))
    head = (lax.broadcasted_iota(jnp.int32, (s_swa.shape[0], 1), 0) // t_new) % SWA_Q_HEADS
    sink = jnp.zeros(head.shape, F32)
    for hq in range(SWA_Q_HEADS):
        sink = jnp.where(head == hq, sinks_ref[l, hq], sink)
    s_swa = jnp.where(valid, s_swa, -jnp.inf)
    m = jnp.maximum(jnp.max(s_swa, axis=1, keepdims=True), sink)
    e = jnp.exp(s_swa - m)
    p_swa = (e * (1.0 / (jnp.sum(e, axis=1, keepdims=True) + jnp.exp(sink - m)))).astype(BF16)

    mem_valid = (lax.broadcasted_iota(jnp.int32, s_mem.shape, 1) % MEM_HEADS
                 == (lax.broadcasted_iota(jnp.int32, s_mem.shape, 0) // t_new) % MEM_HEADS)
    s_mem = jnp.where(mem_valid, s_mem * MEM_SCALE, -jnp.inf)
    e = jnp.exp(s_mem - jnp.max(s_mem, axis=1, keepdims=True))
    p_mem = (e * (1.0 / jnp.sum(e, axis=1, keepdims=True))).astype(BF16)

    for b in range(nb):
        r = rows_of(b)
        o = jnp.dot(p_swa[b * n_swa:(b + 1) * n_swa, :], vext_all[b].astype(BF16), preferred_element_type=F32)
        o_roll = pltpu.roll(o, SWA_HEAD_DIM, 1)
        for pair in range(SWA_Q_HEADS // 2):
            src_even, src_odd = (o, o_roll) if pair // 2 == 0 else (o_roll, o)
            mix_ref[r, MIX_SWA + pair * LANES:MIX_SWA + (pair + 1) * LANES] = jnp.where(
                low, src_even[rows_of(2 * pair), :], src_odd[rows_of(2 * pair + 1), :])
        o = jnp.dot(p_mem[b * n_mem:(b + 1) * n_mem, :], mv_ref[b].astype(BF16), preferred_element_type=F32)
        for hm in range(MEM_HEADS):
            mix_ref[r, MIX_MEM + hm * MEM_HEAD_DIM:MIX_MEM + (hm + 1) * MEM_HEAD_DIM] = o[rows_of(hm), :]


def _sample_mixer(sinks, proj, pool, conv, kc, vc, mk, mv, w_pool, pool_scale, conv_w, new_states,
                  l, nb, t_new, pos0):
    b = proj.shape[0] // t_new
    slots = nb
    blk = lambda a: pl.BlockSpec((None, nb) + a.shape[2:], lambda i: (l, i) + (0,) * (a.ndim - 2))
    out_sds = lambda a: jax.ShapeDtypeStruct(a.shape, F32)
    n_in = 11
    carried = [] if new_states is None else list(new_states)
    return pl.pallas_call(
        functools.partial(_sample_mixer_kernel, l=l, nb=nb, t_new=t_new, pos0=pos0),
        grid=(b // nb,),
        in_specs=[pl.BlockSpec(memory_space=pltpu.SMEM),
                  pl.BlockSpec((nb * t_new, D_IN), lambda i: (i, 0)),
                  blk(pool), blk(conv), blk(kc), blk(vc), blk(mk), blk(mv),
                  _layer_block(w_pool, l), _layer_block(pool_scale, l), _layer_block(conv_w, l)]
                 + [pl.BlockSpec(memory_space=pl.ANY)] * len(carried),
        input_output_aliases={n_in + k: 1 + k for k in range(len(carried))},
        out_specs=[pl.BlockSpec((nb * t_new, 4 * D_GROUP), lambda i: (i, 0)),
                   blk(pool), blk(conv), blk(kc), blk(vc)],
        out_shape=[jax.ShapeDtypeStruct((b * t_new, 4 * D_GROUP), F32),
                   out_sds(pool), out_sds(conv), out_sds(kc), out_sds(vc)],
        scratch_shapes=[pltpu.VMEM((slots, POOL_HALO + 8, D_GROUP), F32),
                        pltpu.VMEM((slots, CONV_HALO + 8, D_GROUP), F32),
                        pltpu.VMEM((slots, 2 * SWA_WINDOW, SWA_KV_DIM), F32),
                        pltpu.VMEM((slots, 2 * SWA_WINDOW, SWA_KV_DIM), F32),
                        pltpu.VMEM((slots, SWA_Q_HEADS * t_new, LANES), F32),
                        pltpu.VMEM((slots, MEM_HEADS * t_new, MEM_HEAD_DIM), F32)],
        compiler_params=_params("arbitrary"),
        name="sample_mixer",
    )(sinks, proj, pool, conv, kc, vc, mk, mv, w_pool, pool_scale, conv_w, *carried)


def kernel(x_prompt, x_sample, mem_prompt, state_pool, state_conv, cache_swa_k, cache_swa_v,
           cache_mem_k, cache_mem_v, g_mix_pre, w_in, w_pool, pool_scale, conv_w, swa_sinks,
           g_mem, w_mem_kv, w_out, g_mix_post, g_mlp_pre, w_up, w_down, g_mlp_post):
    depth = w_in.shape[0]
    bp, seq, d_model = x_prompt.shape
    bs, t_new, _ = x_sample.shape
    nmem = mem_prompt.shape[1]
    assert w_in.shape[2] == D_IN and d_model == 4 * D_GROUP

    w_in_b, w_mem_b, w_out_b = w_in.astype(BF16), w_mem_kv.astype(BF16), w_out.astype(BF16)
    w_pool_b = w_pool.astype(BF16)
    rows = lambda a: a.reshape(depth, 1, a.shape[-1])
    g_mix_pre, g_mem, g_mix_post = rows(g_mix_pre), rows(g_mem), rows(g_mix_post)
    g_mlp_pre, g_mlp_post, pool_scale = rows(g_mlp_pre), rows(g_mlp_post), rows(pool_scale)

    yp = x_prompt.reshape(bp * seq, d_model)
    ys = x_sample.reshape(bs * t_new, d_model)
    mem = mem_prompt.reshape(bp * nmem, d_model)
    kc = cache_swa_k.reshape(depth, bs, SWA_WINDOW, SWA_KV_DIM)
    vc = cache_swa_v.reshape(depth, bs, SWA_WINDOW, SWA_KV_DIM)
    mkc = cache_mem_k.reshape(depth, bs, nmem * MEM_HEADS, MEM_HEAD_DIM)
    mvc = cache_mem_v.reshape(depth, bs, nmem * MEM_HEADS, MEM_HEAD_DIM)

    outs = [[] for _ in range(6)]
    sample_states = None
    for l in range(depth):
        proj = _norm_matmul(ys, g_mix_pre, w_in_b, l, tm=512)
        mix, *sample_states = _sample_mixer(
            swa_sinks, proj, state_pool, state_conv, kc, vc, mkc, mvc, w_pool_b, pool_scale, conv_w,
            sample_states, l, nb=8, t_new=t_new, pos0=PAST_LEN)
        ys = _matmul_norm_res(mix, w_out_b, g_mix_post, ys, l, tm=512)
        ys, w_up_b, w_down_b = _mlp(ys, g_mlp_pre, w_up, w_down, g_mlp_post, l, tm=bs * t_new, tf=512)

        mk, mv = _norm_matmul(mem, g_mem, w_mem_b, l, tm=512, n_out=2)
        mk, mv = mk.reshape(bp, nmem, D_GROUP), mv.reshape(bp, nmem, D_GROUP)
        yp, pool_p, conv_p, k_p, v_p = _prompt_layer(swa_sinks, yp, g_mix_pre, w_in_b, mk, mv, w_pool_b, pool_scale,
                                                     conv_w, w_out_b, g_mix_post, l, tq=256, seq=seq)
        yp = _mlp(yp, g_mlp_pre, w_up_b, w_down_b, g_mlp_post, l, tm=1024, tf=512)

        for lst, val in zip(outs, (pool_p, conv_p, k_p, v_p, mk, mv)):
            lst.append(val)

    pool_p, conv_p, k_p, v_p, mk, mv = (jnp.stack(o) for o in outs)
    pool_s, conv_s, k_s, v_s = sample_states
    kv_shape = lambda a: a.reshape(a.shape[:2] + (SWA_WINDOW, 2, SWA_HEAD_DIM))
    mem_shape = lambda a: a.reshape(depth, bp, nmem, MEM_HEADS, MEM_HEAD_DIM)
    return (yp.reshape(bp, seq, d_model), ys.reshape(bs, t_new, d_model), pool_p, pool_s, conv_p, conv_s,
            kv_shape(k_p), kv_shape(k_s), kv_shape(v_p), kv_shape(v_s), mem_shape(mk), mem_shape(mv))
```

```python
import functools
import math

import jax
import jax.numpy as jnp
from jax import lax
from jax.experimental import pallas as pl
from jax.experimental.pallas import tpu as pltpu

F32 = jnp.float32
BF16 = jnp.bfloat16

LANES = 128
D_GROUP = 512
POOL_WINDOWS = (2, 4, 8, 16)
POOL_CH = D_GROUP // len(POOL_WINDOWS)
POOL_PAD = max(POOL_WINDOWS) - 1
POOL_HALO = 16
CONV_WIDTH = 3
CONV_HALO = 8
SWA_WINDOW = 128
SWA_HEAD_DIM = 64
SWA_Q_HEADS = 8
SWA_GROUP = 4
SWA_KV_DIM = 128
SWA_SCALE = 1.0 / math.sqrt(SWA_HEAD_DIM)
MEM_HEADS = 4
MEM_HEAD_DIM = D_GROUP // MEM_HEADS
MEM_SCALE = 1.0 / math.sqrt(MEM_HEAD_DIM)
RMS_EPS = 1e-6
PAST_LEN = 8192
OFF_U, OFF_HC, OFF_GB, OFF_GC, OFF_Q = 0, 512, 1024, 1536, 2048
OFF_K, OFF_V, OFF_QM, D_IN = 2560, 2688, 2816, 3328
MIX_POOL, MIX_CONV, MIX_SWA, MIX_MEM = 0, 512, 1024, 1536

VMEM_LIMIT_BYTES = 56 * 1024 * 1024


def _rmsnorm(x, g):
    ms = jnp.mean(x * x, axis=-1, keepdims=True)
    return x * lax.rsqrt(ms + RMS_EPS) * g


def _params(*sem, flags=None):
    return pltpu.CompilerParams(dimension_semantics=sem, vmem_limit_bytes=VMEM_LIMIT_BYTES, flags=flags)


def _layer_block(a, l):
    rest = (0,) * (a.ndim - 1)
    return pl.BlockSpec((None,) + a.shape[1:], lambda *_: (l,) + rest, pipeline_mode=pl.Buffered(1))


def _norm_matmul_kernel(x_ref, g_ref, w_ref, *o_refs):
    h = _rmsnorm(x_ref[...], g_ref[...]).astype(BF16)
    y = jnp.dot(h, w_ref[...], preferred_element_type=F32)
    n = y.shape[1] // len(o_refs)
    for i, o_ref in enumerate(o_refs):
        o_ref[...] = y[:, i * n:(i + 1) * n]


def _norm_matmul(x, g, w, l, tm, n_out=1):
    m, k = x.shape
    n = w.shape[2] // n_out
    outs = pl.pallas_call(
        _norm_matmul_kernel,
        grid=(m // tm,),
        in_specs=[pl.BlockSpec((tm, k), lambda i: (i, 0)), _layer_block(g, l), _layer_block(w, l)],
        out_specs=[pl.BlockSpec((tm, n), lambda i: (i, 0))] * n_out,
        out_shape=[jax.ShapeDtypeStruct((m, n), F32)] * n_out,
        compiler_params=_params("arbitrary"),
        name="norm_matmul",
    )(x, g, w)
    return outs[0] if n_out == 1 else outs


def _matmul_norm_res_kernel(a_ref, w_ref, g_ref, x_ref, o_ref):
    y = jnp.dot(a_ref[...].astype(BF16), w_ref[...], preferred_element_type=F32)
    o_ref[...] = x_ref[...] + _rmsnorm(y, g_ref[...])


def _matmul_norm_res(a, w, g, x, l, tm):
    m, k = a.shape
    n = w.shape[2]
    return pl.pallas_call(
        _matmul_norm_res_kernel,
        grid=(m // tm,),
        in_specs=[pl.BlockSpec((tm, k), lambda i: (i, 0)), _layer_block(w, l), _layer_block(g, l),
                  pl.BlockSpec((tm, n), lambda i: (i, 0))],
        out_specs=pl.BlockSpec((tm, n), lambda i: (i, 0)),
        out_shape=jax.ShapeDtypeStruct((m, n), F32),
        compiler_params=_params("arbitrary"),
        name="matmul_norm_res",
    )(a, w, g, x)


def _mlp_kernel(x_ref, gpre_ref, wup_ref, wdown_ref, gpost_ref, o_ref, *rest, cast_weights):
    f = pl.program_id(1)
    h_ref = rest[-1]

    @pl.when(f == 0)
    def _():
        h_ref[...] = _rmsnorm(x_ref[...], gpre_ref[...]).astype(BF16)
        o_ref[...] = jnp.zeros(o_ref.shape, F32)

    if cast_weights:
        wup_b_ref, wdown_b_ref = rest[:2]
        wup, wdown = wup_ref[...].astype(BF16), wdown_ref[...].astype(BF16)
        wup_b_ref[...] = wup
        wdown_b_ref[...] = wdown
    else:
        wup, wdown = wup_ref[...], wdown_ref[...]
    a = jnp.dot(h_ref[...], wup, preferred_element_type=F32)
    a = jnp.square(jnp.maximum(a, 0.0)).astype(BF16)
    o_ref[...] += jnp.dot(a, wdown, preferred_element_type=F32)

    @pl.when(f == pl.num_programs(1) - 1)
    def _():
        o_ref[...] = x_ref[...] + _rmsnorm(o_ref[...], gpost_ref[...])


def _mlp(x, g_pre, w_up, w_down, g_post, l, tm, tf):
    m, d = x.shape
    cast_weights = w_up.ndim == 3
    ff = w_up.shape[-1]
    if cast_weights:
        assert m == tm, "each weight block must be visited exactly once"
        w_specs = [pl.BlockSpec((None, d, tf), lambda i, f: (l, 0, f)),
                   pl.BlockSpec((None, tf, d), lambda i, f: (l, f, 0))]
    else:
        w_specs = [pl.BlockSpec((d, tf), lambda i, f: (0, f)), pl.BlockSpec((tf, d), lambda i, f: (f, 0))]
    out_specs = [pl.BlockSpec((tm, d), lambda i, f: (i, 0))]
    out_shape = [jax.ShapeDtypeStruct((m, d), F32)]
    if cast_weights:
        out_specs += [pl.BlockSpec((d, tf), lambda i, f: (0, f)), pl.BlockSpec((tf, d), lambda i, f: (f, 0))]
        out_shape += [jax.ShapeDtypeStruct((d, ff), BF16), jax.ShapeDtypeStruct((ff, d), BF16)]
    outs = pl.pallas_call(
        functools.partial(_mlp_kernel, cast_weights=cast_weights),
        grid=(m // tm, ff // tf),
        in_specs=[pl.BlockSpec((tm, d), lambda i, f: (i, 0)), _layer_block(g_pre, l)] + w_specs
                 + [_layer_block(g_post, l)],
        out_specs=out_specs,
        out_shape=out_shape,
        scratch_shapes=[pltpu.VMEM((tm, d), BF16)],
        compiler_params=_params("arbitrary", "arbitrary"),
        name="mlp",
    )(x, g_pre, w_up, w_down, g_post)
    return outs if cast_weights else outs[0]


def _mlp_rolling_kernel(x0_ref, xp_ref, xe_ref, gpre_ref, wup_ref, wdown_ref, gpost_ref, o_ref,
                        ha_ref, hb_ref, acca_ref, accb_ref, *, n_tiles, rs):
    i, f = pl.program_id(0), pl.program_id(1)
    rows = pl.ds(pl.multiple_of(f * rs, rs), rs)

    def pre_norm_slice(h_ref):
        h_ref[rows, :] = _rmsnorm(xp_ref[...], gpre_ref[...]).astype(BF16)

    def finish_slice(acc_ref):
        o_ref[...] = xe_ref[...] + _rmsnorm(acc_ref[rows, :], gpost_ref[...])
        acc_ref[rows, :] = jnp.zeros((rs, acc_ref.shape[1]), F32)

    def matmuls(h_ref, acc_ref):
        a = jnp.dot(h_ref[...], wup_ref[...], preferred_element_type=F32)
        a = jnp.square(jnp.maximum(a, 0.0)).astype(BF16)
        acc_ref[...] += jnp.dot(a, wdown_ref[...], preferred_element_type=F32)

    @pl.when((i == 0) & (f == 0))
    def _():
        ha_ref[...] = _rmsnorm(x0_ref[...], gpre_ref[...]).astype(BF16)
        acca_ref[...] = jnp.zeros(acca_ref.shape, F32)
        accb_ref[...] = jnp.zeros(accb_ref.shape, F32)

    @pl.when((i < n_tiles) & (i % 2 == 0))
    def _():
        pre_norm_slice(hb_ref)
        finish_slice(accb_ref)
        matmuls(ha_ref, acca_ref)

    @pl.when((i < n_tiles) & (i % 2 == 1))
    def _():
        pre_norm_slice(ha_ref)
        finish_slice(acca_ref)
        matmuls(hb_ref, accb_ref)

    @pl.when(i == n_tiles)
    def _():
        finish_slice(accb_ref if n_tiles % 2 == 0 else acca_ref)


def _mlp_rolling(x, g_pre, w_up, w_down, g_post, l, tm, tf):
    m, d = x.shape
    ff = w_up.shape[1]
    n_tiles, n_f = m // tm, ff // tf
    rs = tm // n_f
    assert rs * n_f == tm and n_tiles >= 2

    def slice_of_tile(tile):
        return lambda i, f: (tile(i) * n_f + f, 0)
    next_tile = lambda i: jnp.minimum(i + 1, n_tiles - 1)
    prev_tile = lambda i: jnp.maximum(i - 1, 0)
    out_index = lambda i, f: (jnp.where(i == 0, 0, prev_tile(i) * n_f + f), 0)
    ff_chunk = lambda i, f: jnp.where(i < n_tiles, f, n_f - 1)
    return pl.pallas_call(
        functools.partial(_mlp_rolling_kernel, n_tiles=n_tiles, rs=rs),
        grid=(n_tiles + 1, n_f),
        in_specs=[pl.BlockSpec((tm, d), lambda i, f: (0, 0), pipeline_mode=pl.Buffered(1)),
                  pl.BlockSpec((rs, d), slice_of_tile(next_tile)),
                  pl.BlockSpec((rs, d), out_index),
                  _layer_block(g_pre, l),
                  pl.BlockSpec((d, tf), lambda i, f: (0, ff_chunk(i, f))),
                  pl.BlockSpec((tf, d), lambda i, f: (ff_chunk(i, f), 0)),
                  _layer_block(g_post, l)],
        out_specs=pl.BlockSpec((rs, d), out_index),
        out_shape=jax.ShapeDtypeStruct((m, d), F32),
        scratch_shapes=[pltpu.VMEM((tm, d), BF16), pltpu.VMEM((tm, d), BF16),
                        pltpu.VMEM((tm, d), F32), pltpu.VMEM((tm, d), F32)],
        compiler_params=_params("arbitrary", "arbitrary"),
        name="mlp_rolling",
    )(x, x, x, g_pre, w_up, w_down, g_post)


def _pool_delta(uext_ref, u, pos0):
    rows = u.shape[0]
    uext_ref[POOL_HALO:POOL_HALO + rows, :] = u
    pos = pos0 + lax.broadcasted_iota(jnp.int32, (rows, POOL_CH), 0)
    outs = []
    for gi, w in enumerate(POOL_WINDOWS):
        c0 = gi * POOL_CH
        s = u[:, c0:c0 + POOL_CH]
        for back in range(1, w):
            s = s + uext_ref[POOL_HALO - back:POOL_HALO - back + rows, c0:c0 + POOL_CH]
        cnt = jnp.minimum(pos + 1, w).astype(F32)
        outs.append(s / cnt - u[:, c0:c0 + POOL_CH])
    return outs


def _pool_project(deltas, wpool_ref, pscale_ref):
    outs = []
    for gi, d in enumerate(deltas):
        y = jnp.dot(d.astype(BF16), wpool_ref[gi], preferred_element_type=F32)
        outs.append(y * pscale_ref[:, gi * POOL_CH:(gi + 1) * POOL_CH])
    return outs


def _conv_mixer(cext_ref, hc, gb, gc, convw_ref):
    rows = hc.shape[0]
    cext_ref[CONV_HALO:CONV_HALO + rows, :] = gc * hc
    conv = convw_ref[0:1, :] * cext_ref[CONV_HALO - 2:CONV_HALO - 2 + rows, :]
    for kk in range(1, CONV_WIDTH):
        lo = CONV_HALO - 2 + kk
        conv = conv + convw_ref[kk:kk + 1, :] * cext_ref[lo:lo + rows, :]
    return gb * conv


def _swa_kv_head(h, q_of_pair, kwin, vwin, sinks_ref, l, first_key):
    nk = kwin.shape[0]
    low = lax.broadcasted_iota(jnp.int32, (nk, LANES), 1) < SWA_HEAD_DIM
    kroll = pltpu.roll(kwin, SWA_HEAD_DIM, 1)
    vroll = pltpu.roll(vwin, SWA_HEAD_DIM, 1)
    ind_a = jnp.where(low, 1.0, 0.0)
    ind_b = 1.0 - ind_a
    if h == 0:
        k_a, k_b = jnp.where(low, kwin, 0.0), jnp.where(low, 0.0, kroll)
        v_a, v_b = jnp.where(low, vwin, 0.0), jnp.where(low, 0.0, vroll)
    else:
        k_a, k_b = jnp.where(low, kroll, 0.0), jnp.where(low, 0.0, kwin)
        v_a, v_b = jnp.where(low, vroll, 0.0), jnp.where(low, 0.0, vwin)
    kst = jnp.concatenate([k_a, k_b], axis=0).astype(BF16)
    vst = jnp.concatenate([jnp.concatenate([v_a, ind_a], axis=1),
                           jnp.concatenate([v_b, ind_b], axis=1)], axis=0).astype(BF16)
    outs = {}
    for r in range(2):
        pair = 2 * h + r
        q = q_of_pair(pair)
        rows = q.shape[0]
        qi = lax.broadcasted_iota(jnp.int32, (rows, 2 * nk), 0)
        kj = lax.broadcasted_iota(jnp.int32, (rows, 2 * nk), 1) & (nk - 1)
        valid = (kj > qi) & (kj <= qi + SWA_WINDOW) & (kj >= first_key)
        s = lax.dot_general((q * SWA_SCALE).astype(BF16), kst, (((1,), (1,)), ((), ())),
                            preferred_element_type=F32)
        s = jnp.where(valid, s, -jnp.inf)
        sink_a = sinks_ref[l, 4 * h + 2 * r]
        sink_b = sinks_ref[l, 4 * h + 2 * r + 1]
        m_a = jnp.maximum(jnp.max(s[:, :nk], axis=1, keepdims=True), sink_a)
        m_b = jnp.maximum(jnp.max(s[:, nk:], axis=1, keepdims=True), sink_b)
        p = jnp.concatenate([jnp.exp(s[:, :nk] - m_a), jnp.exp(s[:, nk:] - m_b)],
                            axis=1).astype(BF16)
        o = jnp.dot(p, vst, preferred_element_type=F32)
        low_q = lax.broadcasted_iota(jnp.int32, (rows, LANES), 1) < SWA_HEAD_DIM
        den = o[:, LANES:] + jnp.where(low_q, jnp.exp(sink_a - m_a), jnp.exp(sink_b - m_b))
        outs[pair] = o[:, :LANES] / den
    return outs


def _mem_head(qm_h, mk_h, mv_h):
    s = lax.dot_general(qm_h.astype(BF16), mk_h, (((1,), (1,)), ((), ())),
                        preferred_element_type=F32) * MEM_SCALE
    p = jnp.exp(s - jnp.max(s, axis=1, keepdims=True)).astype(BF16)
    vext = jnp.concatenate([mv_h, jnp.ones(mv_h.shape, BF16)], axis=1)
    o = jnp.dot(p, vext, preferred_element_type=F32)
    return o[:, :MEM_HEAD_DIM] / o[:, MEM_HEAD_DIM:]


PROJ_CHUNK = 512
PROJ_LEAD = 2
N_PROJ_CHUNKS = -(-D_IN // PROJ_CHUNK)


def _prompt_layer_kernel(sinks_ref, xc_ref, x1_ref, xn_ref, gpre_ref, win_ref, mk_ref, mv_ref, wpool_ref, pscale_ref,
                         convw_ref, wout_ref, gpost_ref,
                         o_ref, npool_ref, nconv_ref, nk_ref, nv_ref,
                         proja_ref, projb_ref, h_ref, mix_ref, uext_ref, cext_ref, kext_ref, vext_ref,
                         *, l, tq, nblk):
    r = pl.program_id(0)
    j = r % nblk

    def project_chunk(dst_ref, c):
        cols = slice(c * PROJ_CHUNK, min((c + 1) * PROJ_CHUNK, D_IN))
        dst_ref[:, cols] = jnp.dot(h_ref[...], win_ref[:, cols], preferred_element_type=F32)

    @pl.when(r == 0)
    def _():
        h_ref[...] = _rmsnorm(xc_ref[...], gpre_ref[...]).astype(BF16)
        for c in range(N_PROJ_CHUNKS):
            project_chunk(proja_ref, c)
        h_ref[...] = _rmsnorm(x1_ref[...], gpre_ref[...]).astype(BF16)

    @pl.when(j == 0)
    def _():
        uext_ref[0:POOL_HALO, :] = jnp.zeros((POOL_HALO, D_GROUP), F32)
        cext_ref[0:CONV_HALO, :] = jnp.zeros((CONV_HALO, D_GROUP), F32)
        kext_ref[0:SWA_WINDOW, :] = jnp.zeros((SWA_WINDOW, SWA_KV_DIM), F32)
        vext_ref[0:SWA_WINDOW, :] = jnp.zeros((SWA_WINDOW, SWA_KV_DIM), F32)

    def put(c0, y, rows=slice(None)):
        mix_ref[rows, c0:c0 + y.shape[1]] = y.astype(BF16)

    def mix_block(p_ref, pn_ref):
        proj_chunks = [functools.partial(project_chunk, pn_ref, c) for c in range(N_PROJ_CHUNKS)]

        def pool_piece():
            u = p_ref[:, OFF_U:OFF_U + D_GROUP]
            for gi, y in enumerate(_pool_project(_pool_delta(uext_ref, u, j * tq), wpool_ref, pscale_ref)):
                put(MIX_POOL + gi * POOL_CH, y)
            npool_ref[...] = uext_ref[tq + POOL_HALO - POOL_PAD:tq + POOL_HALO, :]
            uext_ref[0:POOL_HALO, :] = uext_ref[tq:tq + POOL_HALO, :]

        def conv_piece():
            y = _conv_mixer(cext_ref, p_ref[:, OFF_HC:OFF_HC + D_GROUP], p_ref[:, OFF_GB:OFF_GB + D_GROUP],
                            p_ref[:, OFF_GC:OFF_GC + D_GROUP], convw_ref)
            put(MIX_CONV, y)
            nconv_ref[...] = cext_ref[tq + CONV_HALO - 2:tq + CONV_HALO, :]
            cext_ref[0:CONV_HALO, :] = cext_ref[tq:tq + CONV_HALO, :]

        def kv_fill_piece():
            kext_ref[SWA_WINDOW:SWA_WINDOW + tq, :] = p_ref[:, OFF_K:OFF_K + SWA_KV_DIM]
            vext_ref[SWA_WINDOW:SWA_WINDOW + tq, :] = p_ref[:, OFF_V:OFF_V + SWA_KV_DIM]
            nk_ref[...] = kext_ref[tq:tq + SWA_WINDOW, :]
            nv_ref[...] = vext_ref[tq:tq + SWA_WINDOW, :]

        def swa_piece(sb, h):
            r0 = sb * SWA_WINDOW
            rows = slice(r0, r0 + SWA_WINDOW)
            first_key = jnp.maximum(SWA_WINDOW - (j * tq + r0), 0)
            q_of_pair = lambda pair: p_ref[rows, OFF_Q + pair * LANES:OFF_Q + (pair + 1) * LANES]
            outs = _swa_kv_head(h, q_of_pair, kext_ref[r0:r0 + 2 * SWA_WINDOW, :],
                                vext_ref[r0:r0 + 2 * SWA_WINDOW, :], sinks_ref, l, first_key)
            for pair, y in outs.items():
                put(MIX_SWA + pair * LANES, y, rows)

        def kv_carry_piece():
            kext_ref[0:SWA_WINDOW, :] = kext_ref[tq:tq + SWA_WINDOW, :]
            vext_ref[0:SWA_WINDOW, :] = vext_ref[tq:tq + SWA_WINDOW, :]

        def mem_piece(hm):
            c0 = hm * MEM_HEAD_DIM
            y = _mem_head(p_ref[:, OFF_QM + c0:OFF_QM + c0 + MEM_HEAD_DIM],
                          mk_ref[:, c0:c0 + MEM_HEAD_DIM].astype(BF16), mv_ref[:, c0:c0 + MEM_HEAD_DIM].astype(BF16))
            put(MIX_MEM + c0, y)

        pieces = [pool_piece, conv_piece, kv_fill_piece]
        pieces += [functools.partial(swa_piece, sb, h) for sb in range(tq // SWA_WINDOW) for h in range(2)]
        pieces += [kv_carry_piece] + [functools.partial(mem_piece, hm) for hm in range(MEM_HEADS)]
        for c in range(PROJ_LEAD):
            proj_chunks[c]()
        rest = proj_chunks[PROJ_LEAD:]
        for i, piece in enumerate(pieces):
            piece()
            for c in range(i * len(rest) // len(pieces), (i + 1) * len(rest) // len(pieces)):
                rest[c]()

        half = tq // 2
        y0 = jnp.dot(mix_ref[0:half, :], wout_ref[...], preferred_element_type=F32)
        h_ref[...] = _rmsnorm(xn_ref[...], gpre_ref[...]).astype(BF16)
        y1 = jnp.dot(mix_ref[half:tq, :], wout_ref[...], preferred_element_type=F32)
        o_ref[0:half, :] = xc_ref[0:half, :] + _rmsnorm(y0, gpost_ref[...])
        o_ref[half:tq, :] = xc_ref[half:tq, :] + _rmsnorm(y1, gpost_ref[...])

    @pl.when(r % 2 == 0)
    def _():
        mix_block(proja_ref, projb_ref)

    @pl.when(r % 2 == 1)
    def _():
        mix_block(projb_ref, proja_ref)


def _prompt_layer(sinks, x, g_pre, w_in, mk, mv, w_pool, pool_scale, conv_w, w_out, g_post, l, tq, seq):
    m, d = x.shape
    nblk, nsteps, b = seq // tq, m // tq, m // seq
    nmem = mk.shape[1]
    per_seq = lambda r: (r // nblk, 0, 0)
    return pl.pallas_call(
        functools.partial(_prompt_layer_kernel, l=l, tq=tq, nblk=nblk),
        grid=(nsteps,),
        in_specs=[pl.BlockSpec(memory_space=pltpu.SMEM),
                  pl.BlockSpec((tq, d), lambda r: (r, 0)),
                  pl.BlockSpec((tq, d), lambda r: (1, 0), pipeline_mode=pl.Buffered(1)),
                  pl.BlockSpec((tq, d), lambda r: (jnp.minimum(r + 2, nsteps - 1), 0)),
                  _layer_block(g_pre, l), _layer_block(w_in, l),
                  pl.BlockSpec((None, nmem, D_GROUP), per_seq),
                  pl.BlockSpec((None, nmem, D_GROUP), per_seq),
                  _layer_block(w_pool, l), _layer_block(pool_scale, l), _layer_block(conv_w, l),
                  _layer_block(w_out, l), _layer_block(g_post, l)],
        out_specs=[pl.BlockSpec((tq, d), lambda r: (r, 0)),
                   pl.BlockSpec((None, POOL_PAD, D_GROUP), per_seq),
                   pl.BlockSpec((None, CONV_WIDTH - 1, D_GROUP), per_seq),
                   pl.BlockSpec((None, SWA_WINDOW, SWA_KV_DIM), per_seq),
                   pl.BlockSpec((None, SWA_WINDOW, SWA_KV_DIM), per_seq)],
        out_shape=[jax.ShapeDtypeStruct((m, d), F32),
                   jax.ShapeDtypeStruct((b, POOL_PAD, D_GROUP), F32),
                   jax.ShapeDtypeStruct((b, CONV_WIDTH - 1, D_GROUP), F32),
                   jax.ShapeDtypeStruct((b, SWA_WINDOW, SWA_KV_DIM), F32),
                   jax.ShapeDtypeStruct((b, SWA_WINDOW, SWA_KV_DIM), F32)],
        scratch_shapes=[pltpu.VMEM((tq, D_IN), F32),
                        pltpu.VMEM((tq, D_IN), F32),
                        pltpu.VMEM((tq, d), BF16),
                        pltpu.VMEM((tq, d), BF16),
                        pltpu.VMEM((POOL_HALO + tq, D_GROUP), F32),
                        pltpu.VMEM((CONV_HALO + tq, D_GROUP), F32),
                        pltpu.VMEM((SWA_WINDOW + tq, SWA_KV_DIM), F32),
                        pltpu.VMEM((SWA_WINDOW + tq, SWA_KV_DIM), F32)],
        compiler_params=_params("arbitrary"),
        name="prompt_layer",
    )(sinks, x, x, x, g_pre, w_in, mk, mv, w_pool, pool_scale, conv_w, w_out, g_post)


def _sample_mixer_kernel(sinks_ref, proj_ref, pool_ref, conv_ref, kc_ref, vc_ref, mk_ref, mv_ref,
                         wpool_ref, pscale_ref, convw_ref, *refs, l, nb, t_new, pos0):
    mix_ref, npool_ref, nconv_ref, nk_ref, nv_ref = refs[-11:-6]
    uext_all, cext_all, kext_all, vext_all, qs_all, qm_all = refs[-6:]
    uext_all[...] = jnp.zeros(uext_all.shape, F32)
    cext_all[...] = jnp.zeros(cext_all.shape, F32)
    kext_all[...] = jnp.zeros(kext_all.shape, F32)
    vext_all[...] = jnp.zeros(vext_all.shape, F32)

    n_swa = SWA_Q_HEADS * t_new
    n_mem = MEM_HEADS * t_new
    low = lax.broadcasted_iota(jnp.int32, (t_new, LANES), 1) < SWA_HEAD_DIM
    rows_of = lambda a: slice(a * t_new, (a + 1) * t_new)

    deltas = []
    for b in range(nb):
        r = rows_of(b)
        uext_ref, cext_ref, kext_ref, vext_ref = uext_all.at[b], cext_all.at[b], kext_all.at[b], vext_all.at[b]
        uext_ref[POOL_HALO - POOL_PAD:POOL_HALO, :] = pool_ref[b]
        deltas.append(_pool_delta(uext_ref, proj_ref[r, OFF_U:OFF_U + D_GROUP], pos0))
        npool_ref[b] = uext_ref[t_new + POOL_HALO - POOL_PAD:t_new + POOL_HALO, :]

        cext_ref[CONV_HALO - 2:CONV_HALO, :] = conv_ref[b]
        mix_ref[r, MIX_CONV:MIX_CONV + D_GROUP] = _conv_mixer(
            cext_ref, proj_ref[r, OFF_HC:OFF_HC + D_GROUP], proj_ref[r, OFF_GB:OFF_GB + D_GROUP],
            proj_ref[r, OFF_GC:OFF_GC + D_GROUP], convw_ref)
        nconv_ref[b] = cext_ref[t_new + CONV_HALO - 2:t_new + CONV_HALO, :]

        kext_ref[0:SWA_WINDOW, :] = kc_ref[b]
        vext_ref[0:SWA_WINDOW, :] = vc_ref[b]
        kext_ref[SWA_WINDOW:SWA_WINDOW + t_new, :] = proj_ref[r, OFF_K:OFF_K + SWA_KV_DIM]
        vext_ref[SWA_WINDOW:SWA_WINDOW + t_new, :] = proj_ref[r, OFF_V:OFF_V + SWA_KV_DIM]
        nk_ref[b] = kext_ref[t_new:t_new + SWA_WINDOW, :]
        nv_ref[b] = vext_ref[t_new:t_new + SWA_WINDOW, :]
        for pair in range(SWA_Q_HEADS // 2):
            qp = proj_ref[r, OFF_Q + pair * LANES:OFF_Q + (pair + 1) * LANES] * SWA_SCALE
            qr = pltpu.roll(qp, SWA_HEAD_DIM, 1)
            if pair // 2 == 0:
                q_even, q_odd = jnp.where(low, qp, 0.0), jnp.where(low, qr, 0.0)
            else:
                q_even, q_odd = jnp.where(low, 0.0, qr), jnp.where(low, 0.0, qp)
            qs_all[b, rows_of(2 * pair), :] = q_even
            qs_all[b, rows_of(2 * pair + 1), :] = q_odd
        for hm in range(MEM_HEADS):
            qm_all[b, rows_of(hm), :] = (
                proj_ref[r, OFF_QM + hm * MEM_HEAD_DIM:OFF_QM + (hm + 1) * MEM_HEAD_DIM])

    deltas = [jnp.concatenate([d[gi] for d in deltas], axis=0) for gi in range(len(POOL_WINDOWS))]
    for gi, y in enumerate(_pool_project(deltas, wpool_ref, pscale_ref)):
        mix_ref[:, MIX_POOL + gi * POOL_CH:MIX_POOL + (gi + 1) * POOL_CH] = y

    nt_dims = (((1,), (1,)), ((), ()))
    s_swa = jnp.concatenate(
        [lax.dot_general(qs_all[b].astype(BF16), kext_all[b].astype(BF16),
                         nt_dims, preferred_element_type=F32) for b in range(nb)], axis=0)
    s_mem = jnp.concatenate(
        [lax.dot_general(qm_all[b].astype(BF16), mk_ref[b].astype(BF16),
                         nt_dims, preferred_element_type=F32) for b in range(nb)], axis=0)

    ri = lax.broadcasted_iota(jnp.int32, s_swa.shape, 0)
    tok = ri % t_new
    kj = lax.broadcasted_iota(jnp.int32, s_swa.shape, 1)
    valid = (kj > tok) & (kj <= tok + SWA_WINDOW) & (kj >= max(SWA_WINDOW - pos0, 0))
    head = (lax.broadcasted_iota(jnp.int32, (s_swa.shape[0], 1), 0) // t_new) % SWA_Q_HEADS
    sink = jnp.zeros(head.shape, F32)
    for hq in range(SWA_Q_HEADS):
        sink = jnp.where(head == hq, sinks_ref[l, hq], sink)
    s_swa = jnp.where(valid, s_swa, -jnp.inf)
    m = jnp.maximum(jnp.max(s_swa, axis=1, keepdims=True), sink)
    e = jnp.exp(s_swa - m)
    p_swa = (e * (1.0 / (jnp.sum(e, axis=1, keepdims=True) + jnp.exp(sink - m)))).astype(BF16)

    mem_valid = (lax.broadcasted_iota(jnp.int32, s_mem.shape, 1) % MEM_HEADS
                 == (lax.broadcasted_iota(jnp.int32, s_mem.shape, 0) // t_new) % MEM_HEADS)
    s_mem = jnp.where(mem_valid, s_mem * MEM_SCALE, -jnp.inf)
    e = jnp.exp(s_mem - jnp.max(s_mem, axis=1, keepdims=True))
    p_mem = (e * (1.0 / jnp.sum(e, axis=1, keepdims=True))).astype(BF16)

    for b in range(nb):
        r = rows_of(b)
        o = jnp.dot(p_swa[b * n_swa:(b + 1) * n_swa, :], vext_all[b].astype(BF16), preferred_element_type=F32)
        o_roll = pltpu.roll(o, SWA_HEAD_DIM, 1)
        for pair in range(SWA_Q_HEADS // 2):
            src_even, src_odd = (o, o_roll) if pair // 2 == 0 else (o_roll, o)
            mix_ref[r, MIX_SWA + pair * LANES:MIX_SWA + (pair + 1) * LANES] = jnp.where(
                low, src_even[rows_of(2 * pair), :], src_odd[rows_of(2 * pair + 1), :])
        o = jnp.dot(p_mem[b * n_mem:(b + 1) * n_mem, :], mv_ref[b].astype(BF16), preferred_element_type=F32)
        for hm in range(MEM_HEADS):
            mix_ref[r, MIX_MEM + hm * MEM_HEAD_DIM:MIX_MEM + (hm + 1) * MEM_HEAD_DIM] = o[rows_of(hm), :]


def _sample_mixer(sinks, proj, pool, conv, kc, vc, mk, mv, w_pool, pool_scale, conv_w, new_states,
                  l, nb, t_new, pos0):
    b = proj.shape[0] // t_new
    slots = nb
    blk = lambda a: pl.BlockSpec((None, nb) + a.shape[2:], lambda i: (l, i) + (0,) * (a.ndim - 2))
    out_sds = lambda a: jax.ShapeDtypeStruct(a.shape, F32)
    n_in = 11
    carried = [] if new_states is None else list(new_states)
    return pl.pallas_call(
        functools.partial(_sample_mixer_kernel, l=l, nb=nb, t_new=t_new, pos0=pos0),
        grid=(b // nb,),
        in_specs=[pl.BlockSpec(memory_space=pltpu.SMEM),
                  pl.BlockSpec((nb * t_new, D_IN), lambda i: (i, 0)),
                  blk(pool), blk(conv), blk(kc), blk(vc), blk(mk), blk(mv),
                  _layer_block(w_pool, l), _layer_block(pool_scale, l), _layer_block(conv_w, l)]
                 + [pl.BlockSpec(memory_space=pl.ANY)] * len(carried),
        input_output_aliases={n_in + k: 1 + k for k in range(len(carried))},
        out_specs=[pl.BlockSpec((nb * t_new, 4 * D_GROUP), lambda i: (i, 0)),
                   blk(pool), blk(conv), blk(kc), blk(vc)],
        out_shape=[jax.ShapeDtypeStruct((b * t_new, 4 * D_GROUP), F32),
                   out_sds(pool), out_sds(conv), out_sds(kc), out_sds(vc)],
        scratch_shapes=[pltpu.VMEM((slots, POOL_HALO + 8, D_GROUP), F32),
                        pltpu.VMEM((slots, CONV_HALO + 8, D_GROUP), F32),
                        pltpu.VMEM((slots, 2 * SWA_WINDOW, SWA_KV_DIM), F32),
                        pltpu.VMEM((slots, 2 * SWA_WINDOW, SWA_KV_DIM), F32),
                        pltpu.VMEM((slots, SWA_Q_HEADS * t_new, LANES), F32),
                        pltpu.VMEM((slots, MEM_HEADS * t_new, MEM_HEAD_DIM), F32)],
        compiler_params=_params("arbitrary"),
        name="sample_mixer",
    )(sinks, proj, pool, conv, kc, vc, mk, mv, w_pool, pool_scale, conv_w, *carried)


def kernel(x_prompt, x_sample, mem_prompt, state_pool, state_conv, cache_swa_k, cache_swa_v,
           cache_mem_k, cache_mem_v, g_mix_pre, w_in, w_pool, pool_scale, conv_w, swa_sinks,
           g_mem, w_mem_kv, w_out, g_mix_post, g_mlp_pre, w_up, w_down, g_mlp_post):
    depth = w_in.shape[0]
    bp, seq, d_model = x_prompt.shape
    bs, t_new, _ = x_sample.shape
    nmem = mem_prompt.shape[1]
    assert w_in.shape[2] == D_IN and d_model == 4 * D_GROUP

    w_in_b, w_mem_b, w_out_b = w_in.astype(BF16), w_mem_kv.astype(BF16), w_out.astype(BF16)
    w_pool_b = w_pool.astype(BF16)
    rows = lambda a: a.reshape(depth, 1, a.shape[-1])
    g_mix_pre, g_mem, g_mix_post = rows(g_mix_pre), rows(g_mem), rows(g_mix_post)
    g_mlp_pre, g_mlp_post, pool_scale = rows(g_mlp_pre), rows(g_mlp_post), rows(pool_scale)

    yp = x_prompt.reshape(bp * seq, d_model)
    ys = x_sample.reshape(bs * t_new, d_model)
    mem = mem_prompt.reshape(bp * nmem, d_model)
    kc = cache_swa_k.reshape(depth, bs, SWA_WINDOW, SWA_KV_DIM)
    vc = cache_swa_v.reshape(depth, bs, SWA_WINDOW, SWA_KV_DIM)
    mkc = cache_mem_k.reshape(depth, bs, nmem * MEM_HEADS, MEM_HEAD_DIM)
    mvc = cache_mem_v.reshape(depth, bs, nmem * MEM_HEADS, MEM_HEAD_DIM)

    outs = [[] for _ in range(6)]
    sample_states = None
    for l in range(depth):
        proj = _norm_matmul(ys, g_mix_pre, w_in_b, l, tm=512)
        mix, *sample_states = _sample_mixer(
            swa_sinks, proj, state_pool, state_conv, kc, vc, mkc, mvc, w_pool_b, pool_scale, conv_w,
            sample_states, l, nb=8, t_new=t_new, pos0=PAST_LEN)
        ys = _matmul_norm_res(mix, w_out_b, g_mix_post, ys, l, tm=512)
        ys, w_up_b, w_down_b = _mlp(ys, g_mlp_pre, w_up, w_down, g_mlp_post, l, tm=bs * t_new, tf=512)

        mk, mv = _norm_matmul(mem, g_mem, w_mem_b, l, tm=512, n_out=2)
        mk, mv = mk.reshape(bp, nmem, D_GROUP), mv.reshape(bp, nmem, D_GROUP)
        yp, pool_p, conv_p, k_p, v_p = _prompt_layer(swa_sinks, yp, g_mix_pre, w_in_b, mk, mv, w_pool_b, pool_scale,
                                                     conv_w, w_out_b, g_mix_post, l, tq=256, seq=seq)
        yp = _mlp_rolling(yp, g_mlp_pre, w_up_b, w_down_b, g_mlp_post, l, tm=1024, tf=512)

        for lst, val in zip(outs, (pool_p, conv_p, k_p, v_p, mk, mv)):
            lst.append(val)

    pool_p, conv_p, k_p, v_p, mk, mv = (jnp.stack(o) for o in outs)
    pool_s, conv_s, k_s, v_s = sample_states
    kv_shape = lambda a: a.reshape(a.shape[:2] + (SWA_WINDOW, 2, SWA_HEAD_DIM))
    mem_shape = lambda a: a.reshape(depth, bp, nmem, MEM_HEADS, MEM_HEAD_DIM)
    return (yp.reshape(bp, seq, d_model), ys.reshape(bs, t_new, d_model), pool_p, pool_s, conv_p, conv_s,
            kv_shape(k_p), kv_shape(k_s), kv_shape(v_p), kv_shape(v_s), mem_shape(mk), mem_shape(mv))
```

```python
import functools
import math

import jax
import jax.numpy as jnp
from jax import lax
from jax.experimental import pallas as pl
from jax.experimental.pallas import tpu as pltpu

F32 = jnp.float32
BF16 = jnp.bfloat16

LANES = 128
D_GROUP = 512
POOL_WINDOWS = (2, 4, 8, 16)
POOL_CH = D_GROUP // len(POOL_WINDOWS)
POOL_PAD = max(POOL_WINDOWS) - 1
POOL_HALO = 16
CONV_WIDTH = 3
CONV_HALO = 8
SWA_WINDOW = 128
SWA_HEAD_DIM = 64
SWA_Q_HEADS = 8
SWA_GROUP = 4
SWA_KV_DIM = 128
SWA_SCALE = 1.0 / math.sqrt(SWA_HEAD_DIM)
MEM_HEADS = 4
MEM_HEAD_DIM = D_GROUP // MEM_HEADS
MEM_SCALE = 1.0 / math.sqrt(MEM_HEAD_DIM)
RMS_EPS = 1e-6
PAST_LEN = 8192
OFF_U, OFF_HC, OFF_GB, OFF_GC, OFF_Q = 0, 512, 1024, 1536, 2048
OFF_K, OFF_V, OFF_QM, D_IN = 2560, 2688, 2816, 3328
MIX_POOL, MIX_CONV, MIX_SWA, MIX_MEM = 0, 512, 1024, 1536

VMEM_LIMIT_BYTES = 56 * 1024 * 1024
PROMPT_LAYER_VMEM_BYTES = 60 * 1024 * 1024


def _rmsnorm(x, g):
    ms = jnp.mean(x * x, axis=-1, keepdims=True)
    return x * lax.rsqrt(ms + RMS_EPS) * g


def _params(*sem, vmem=VMEM_LIMIT_BYTES):
    return pltpu.CompilerParams(dimension_semantics=sem, vmem_limit_bytes=vmem)


def _layer_block(a, l):
    rest = (0,) * (a.ndim - 1)
    return pl.BlockSpec((None,) + a.shape[1:], lambda *_: (l,) + rest, pipeline_mode=pl.Buffered(1))


def _norm_matmul_kernel(x_ref, g_ref, w_ref, *o_refs):
    h = _rmsnorm(x_ref[...], g_ref[...]).astype(BF16)
    w = w_ref[...]
    if w.dtype != BF16:
        w = w.astype(BF16)
    y = jnp.dot(h, w, preferred_element_type=F32)
    n = y.shape[1] // len(o_refs)
    for i, o_ref in enumerate(o_refs):
        o_ref[...] = y[:, i * n:(i + 1) * n]


def _norm_matmul(x, g, w, l, tm, n_out=1):
    m, k = x.shape
    n = w.shape[2] // n_out
    outs = pl.pallas_call(
        _norm_matmul_kernel,
        grid=(m // tm,),
        in_specs=[pl.BlockSpec((tm, k), lambda i: (i, 0)), _layer_block(g, l), _layer_block(w, l)],
        out_specs=[pl.BlockSpec((tm, n), lambda i: (i, 0))] * n_out,
        out_shape=[jax.ShapeDtypeStruct((m, n), F32)] * n_out,
        compiler_params=_params("arbitrary"),
        name="norm_matmul",
    )(x, g, w)
    return outs[0] if n_out == 1 else outs


def _matmul_norm_res_kernel(a_ref, w_ref, g_ref, x_ref, o_ref):
    y = jnp.dot(a_ref[...].astype(BF16), w_ref[...], preferred_element_type=F32)
    o_ref[...] = x_ref[...] + _rmsnorm(y, g_ref[...])


def _matmul_norm_res(a, w, g, x, l, tm):
    m, k = a.shape
    n = w.shape[2]
    return pl.pallas_call(
        _matmul_norm_res_kernel,
        grid=(m // tm,),
        in_specs=[pl.BlockSpec((tm, k), lambda i: (i, 0)), _layer_block(w, l), _layer_block(g, l),
                  pl.BlockSpec((tm, n), lambda i: (i, 0))],
        out_specs=pl.BlockSpec((tm, n), lambda i: (i, 0)),
        out_shape=jax.ShapeDtypeStruct((m, n), F32),
        compiler_params=_params("arbitrary"),
        name="matmul_norm_res",
    )(a, w, g, x)


def _mlp_kernel(x_ref, gpre_ref, wup_ref, wdown_ref, gpost_ref, o_ref, *rest, cast_weights):
    f = pl.program_id(1)
    h_ref = rest[-1]

    @pl.when(f == 0)
    def _():
        h_ref[...] = _rmsnorm(x_ref[...], gpre_ref[...]).astype(BF16)
        o_ref[...] = jnp.zeros(o_ref.shape, F32)

    if cast_weights:
        wup_b_ref, wdown_b_ref = rest[:2]
        wup, wdown = wup_ref[...].astype(BF16), wdown_ref[...].astype(BF16)
        wup_b_ref[...] = wup
        wdown_b_ref[...] = wdown
    else:
        wup, wdown = wup_ref[...], wdown_ref[...]
    a = jnp.dot(h_ref[...], wup, preferred_element_type=F32)
    a = jnp.square(jnp.maximum(a, 0.0)).astype(BF16)
    o_ref[...] += jnp.dot(a, wdown, preferred_element_type=F32)

    @pl.when(f == pl.num_programs(1) - 1)
    def _():
        o_ref[...] = x_ref[...] + _rmsnorm(o_ref[...], gpost_ref[...])


def _mlp(x, g_pre, w_up, w_down, g_post, l, tm, tf):
    m, d = x.shape
    cast_weights = w_up.ndim == 3
    ff = w_up.shape[-1]
    if cast_weights:
        assert m == tm, "each weight block must be visited exactly once"
        w_specs = [pl.BlockSpec((None, d, tf), lambda i, f: (l, 0, f)),
                   pl.BlockSpec((None, tf, d), lambda i, f: (l, f, 0))]
    else:
        w_specs = [pl.BlockSpec((d, tf), lambda i, f: (0, f)), pl.BlockSpec((tf, d), lambda i, f: (f, 0))]
    out_specs = [pl.BlockSpec((tm, d), lambda i, f: (i, 0))]
    out_shape = [jax.ShapeDtypeStruct((m, d), F32)]
    if cast_weights:
        out_specs += [pl.BlockSpec((d, tf), lambda i, f: (0, f)), pl.BlockSpec((tf, d), lambda i, f: (f, 0))]
        out_shape += [jax.ShapeDtypeStruct((d, ff), BF16), jax.ShapeDtypeStruct((ff, d), BF16)]
    outs = pl.pallas_call(
        functools.partial(_mlp_kernel, cast_weights=cast_weights),
        grid=(m // tm, ff // tf),
        in_specs=[pl.BlockSpec((tm, d), lambda i, f: (i, 0)), _layer_block(g_pre, l)] + w_specs
                 + [_layer_block(g_post, l)],
        out_specs=out_specs,
        out_shape=out_shape,
        scratch_shapes=[pltpu.VMEM((tm, d), BF16)],
        compiler_params=_params("arbitrary", "arbitrary"),
        name="mlp",
    )(x, g_pre, w_up, w_down, g_post)
    return outs if cast_weights else outs[0]


def _pool_delta(uext_ref, u, pos0):
    rows = u.shape[0]
    uext_ref[POOL_HALO:POOL_HALO + rows, :] = u
    return [_pool_group_delta(uext_ref, rows, pos0, gi) for gi in range(len(POOL_WINDOWS))]


def _pool_group_delta(uext_ref, rows, pos0, gi):
    w = POOL_WINDOWS[gi]
    cols = slice(gi * POOL_CH, (gi + 1) * POOL_CH)
    u = uext_ref[POOL_HALO:POOL_HALO + rows, cols]
    s = u
    for back in range(1, w):
        s = s + uext_ref[POOL_HALO - back:POOL_HALO - back + rows, cols]
    pos = pos0 + lax.broadcasted_iota(jnp.int32, (rows, POOL_CH), 0)
    cnt = jnp.minimum(pos + 1, w).astype(F32)
    return s / cnt - u


def _pool_group_project(gi, d, wpool_ref, pscale_ref):
    y = jnp.dot(d.astype(BF16), wpool_ref[gi], preferred_element_type=F32)
    return y * pscale_ref[:, gi * POOL_CH:(gi + 1) * POOL_CH]


def _pool_project(deltas, wpool_ref, pscale_ref):
    return [_pool_group_project(gi, d, wpool_ref, pscale_ref) for gi, d in enumerate(deltas)]


def _conv_mixer(cext_ref, hc, gb, gc, convw_ref):
    rows = hc.shape[0]
    cext_ref[CONV_HALO:CONV_HALO + rows, :] = gc * hc
    conv = convw_ref[0:1, :] * cext_ref[CONV_HALO - 2:CONV_HALO - 2 + rows, :]
    for kk in range(1, CONV_WIDTH):
        lo = CONV_HALO - 2 + kk
        conv = conv + convw_ref[kk:kk + 1, :] * cext_ref[lo:lo + rows, :]
    return gb * conv


def _swa_kv_head(h, q_of_pair, kwin, vwin, sinks_ref, l, first_key):
    nk = kwin.shape[0]
    low = lax.broadcasted_iota(jnp.int32, (nk, LANES), 1) < SWA_HEAD_DIM
    kroll = pltpu.roll(kwin, SWA_HEAD_DIM, 1)
    vroll = pltpu.roll(vwin, SWA_HEAD_DIM, 1)
    ind_a = jnp.where(low, 1.0, 0.0)
    ind_b = 1.0 - ind_a
    if h == 0:
        k_a, k_b = jnp.where(low, kwin, 0.0), jnp.where(low, 0.0, kroll)
        v_a, v_b = jnp.where(low, vwin, 0.0), jnp.where(low, 0.0, vroll)
    else:
        k_a, k_b = jnp.where(low, kroll, 0.0), jnp.where(low, 0.0, kwin)
        v_a, v_b = jnp.where(low, vroll, 0.0), jnp.where(low, 0.0, vwin)
    kst = jnp.concatenate([k_a, k_b], axis=0).astype(BF16)
    vst = jnp.concatenate([jnp.concatenate([v_a, ind_a], axis=1),
                           jnp.concatenate([v_b, ind_b], axis=1)], axis=0).astype(BF16)
    outs = {}
    for r in range(2):
        pair = 2 * h + r
        q = q_of_pair(pair)
        rows = q.shape[0]
        qi = lax.broadcasted_iota(jnp.int32, (rows, 2 * nk), 0)
        kj = lax.broadcasted_iota(jnp.int32, (rows, 2 * nk), 1) & (nk - 1)
        valid = (kj > qi) & (kj <= qi + SWA_WINDOW) & (kj >= first_key)
        s = lax.dot_general((q * SWA_SCALE).astype(BF16), kst, (((1,), (1,)), ((), ())),
                            preferred_element_type=F32)
        s = jnp.where(valid, s, -jnp.inf)
        sink_a = sinks_ref[l, 4 * h + 2 * r]
        sink_b = sinks_ref[l, 4 * h + 2 * r + 1]
        m_a = jnp.maximum(jnp.max(s[:, :nk], axis=1, keepdims=True), sink_a)
        m_b = jnp.maximum(jnp.max(s[:, nk:], axis=1, keepdims=True), sink_b)
        p = jnp.concatenate([jnp.exp(s[:, :nk] - m_a), jnp.exp(s[:, nk:] - m_b)],
                            axis=1).astype(BF16)
        o = jnp.dot(p, vst, preferred_element_type=F32)
        low_q = lax.broadcasted_iota(jnp.int32, (rows, LANES), 1) < SWA_HEAD_DIM
        den = o[:, LANES:] + jnp.where(low_q, jnp.exp(sink_a - m_a), jnp.exp(sink_b - m_b))
        outs[pair] = o[:, :LANES] / den
    return outs


def _mem_head(qm_h, mk_h, mv_h):
    s = lax.dot_general(qm_h.astype(BF16), mk_h, (((1,), (1,)), ((), ())),
                        preferred_element_type=F32) * MEM_SCALE
    p = jnp.exp(s - jnp.max(s, axis=1, keepdims=True)).astype(BF16)
    vext = jnp.concatenate([mv_h, jnp.ones(mv_h.shape, BF16)], axis=1)
    o = jnp.dot(p, vext, preferred_element_type=F32)
    return o[:, :MEM_HEAD_DIM] / o[:, MEM_HEAD_DIM:]


PROJ_CHUNK = 512
N_PROJ_CHUNKS = -(-D_IN // PROJ_CHUNK)


def _prompt_layer_kernel(sinks_ref, xp_ref, xn_ref, gpre_ref, win_ref, mk_ref, mv_ref, wpool_ref, pscale_ref,
                         convw_ref, wout_ref, gpost_ref,
                         o_ref, npool_ref, nconv_ref, nk_ref, nv_ref,
                         proja_ref, projb_ref, h_ref, mixa_ref, mixb_ref, y_ref, uext_ref, cext_ref, kext_ref,
                         vext_ref, *, l, tq, nblk, nsteps):
    r = pl.program_id(0)
    j = r % nblk
    half = tq // 2

    def project_chunk(dst_ref, c):
        cols = slice(c * PROJ_CHUNK, min((c + 1) * PROJ_CHUNK, D_IN))
        dst_ref[:, cols] = jnp.dot(h_ref[...], win_ref[:, cols], preferred_element_type=F32)

    n_out_chunks = wout_ref.shape[1] // PROJ_CHUNK
    n_norm_parts = 4

    def out_chunk(m_ref, c):
        cols = slice(c * PROJ_CHUNK, (c + 1) * PROJ_CHUNK)
        y_ref[:, cols] = jnp.dot(m_ref[...], wout_ref[:, cols], preferred_element_type=F32)

    def finish_half(k):
        rows = slice(k * half, (k + 1) * half)
        o_ref[rows, :] = xp_ref[rows, :] + _rmsnorm(y_ref[rows, :], gpost_ref[...])

    def norm_part(k):
        rows = slice(k * tq // n_norm_parts, (k + 1) * tq // n_norm_parts)
        h_ref[rows, :] = _rmsnorm(xn_ref[rows, :], gpre_ref[...]).astype(BF16)

    @pl.when(r == 0)
    def _():
        h_ref[...] = _rmsnorm(xp_ref[...], gpre_ref[...]).astype(BF16)
        for c in range(N_PROJ_CHUNKS):
            project_chunk(proja_ref, c)
        mixb_ref[...] = jnp.zeros(mixb_ref.shape, BF16)

    @pl.when(j == 0)
    def _():
        uext_ref[0:POOL_HALO, :] = jnp.zeros((POOL_HALO, D_GROUP), F32)
        cext_ref[0:CONV_HALO, :] = jnp.zeros((CONV_HALO, D_GROUP), F32)
        kext_ref[0:SWA_WINDOW, :] = jnp.zeros((SWA_WINDOW, SWA_KV_DIM), F32)
        vext_ref[0:SWA_WINDOW, :] = jnp.zeros((SWA_WINDOW, SWA_KV_DIM), F32)

    def mix_block(p_ref, pn_ref, m_ref, mp_ref):
        proj_chunks = [functools.partial(project_chunk, pn_ref, c) for c in range(N_PROJ_CHUNKS)]

        def put(c0, y, rows=slice(None)):
            m_ref[rows, c0:c0 + y.shape[1]] = y.astype(BF16)

        def pool_fill_piece():
            uext_ref[POOL_HALO:POOL_HALO + tq, :] = p_ref[:, OFF_U:OFF_U + D_GROUP]
            npool_ref[...] = uext_ref[tq + POOL_HALO - POOL_PAD:tq + POOL_HALO, :]

        def pool_group_piece(gi):
            d = _pool_group_delta(uext_ref, tq, j * tq, gi)
            put(MIX_POOL + gi * POOL_CH, _pool_group_project(gi, d, wpool_ref, pscale_ref))

        def pool_carry_piece():
            uext_ref[0:POOL_HALO, :] = uext_ref[tq:tq + POOL_HALO, :]

        def conv_piece():
            y = _conv_mixer(cext_ref, p_ref[:, OFF_HC:OFF_HC + D_GROUP], p_ref[:, OFF_GB:OFF_GB + D_GROUP],
                            p_ref[:, OFF_GC:OFF_GC + D_GROUP], convw_ref)
            put(MIX_CONV, y)
            nconv_ref[...] = cext_ref[tq + CONV_HALO - 2:tq + CONV_HALO, :]
            cext_ref[0:CONV_HALO, :] = cext_ref[tq:tq + CONV_HALO, :]

        def kv_fill_piece():
            kext_ref[SWA_WINDOW:SWA_WINDOW + tq, :] = p_ref[:, OFF_K:OFF_K + SWA_KV_DIM]
            vext_ref[SWA_WINDOW:SWA_WINDOW + tq, :] = p_ref[:, OFF_V:OFF_V + SWA_KV_DIM]
            nk_ref[...] = kext_ref[tq:tq + SWA_WINDOW, :]
            nv_ref[...] = vext_ref[tq:tq + SWA_WINDOW, :]

        def swa_piece(sb, h):
            r0 = sb * SWA_WINDOW
            rows = slice(r0, r0 + SWA_WINDOW)
            first_key = jnp.maximum(SWA_WINDOW - (j * tq + r0), 0)
            q_of_pair = lambda pair: p_ref[rows, OFF_Q + pair * LANES:OFF_Q + (pair + 1) * LANES]
            outs = _swa_kv_head(h, q_of_pair, kext_ref[r0:r0 + 2 * SWA_WINDOW, :],
                                vext_ref[r0:r0 + 2 * SWA_WINDOW, :], sinks_ref, l, first_key)
            for pair, y in outs.items():
                put(MIX_SWA + pair * LANES, y, rows)

        def kv_carry_piece():
            kext_ref[0:SWA_WINDOW, :] = kext_ref[tq:tq + SWA_WINDOW, :]
            vext_ref[0:SWA_WINDOW, :] = vext_ref[tq:tq + SWA_WINDOW, :]

        def mem_piece(hm):
            c0 = hm * MEM_HEAD_DIM
            y = _mem_head(p_ref[:, OFF_QM + c0:OFF_QM + c0 + MEM_HEAD_DIM],
                          mk_ref[:, c0:c0 + MEM_HEAD_DIM].astype(BF16), mv_ref[:, c0:c0 + MEM_HEAD_DIM].astype(BF16))
            put(MIX_MEM + c0, y)

        mxu_items = [functools.partial(out_chunk, mp_ref, c) for c in range(n_out_chunks)] + proj_chunks
        vpu_items = [functools.partial(norm_part, k) for k in range(n_norm_parts)]
        vpu_items += [pool_fill_piece] + [functools.partial(pool_group_piece, gi) for gi in range(len(POOL_WINDOWS))]
        vpu_items += [pool_carry_piece, functools.partial(finish_half, 0), conv_piece,
                      functools.partial(finish_half, 1), kv_fill_piece]
        vpu_items += [functools.partial(swa_piece, sb, h) for sb in range(tq // SWA_WINDOW) for h in range(2)]
        vpu_items += [kv_carry_piece] + [functools.partial(mem_piece, hm) for hm in range(MEM_HEADS)]
        mxu_items[0]()
        rest = mxu_items[1:]
        for i, item in enumerate(vpu_items):
            item()
            for c in range(i * len(rest) // len(vpu_items), (i + 1) * len(rest) // len(vpu_items)):
                rest[c]()

    @pl.when((r < nsteps) & (r % 2 == 0))
    def _():
        mix_block(proja_ref, projb_ref, mixa_ref, mixb_ref)

    @pl.when((r < nsteps) & (r % 2 == 1))
    def _():
        mix_block(projb_ref, proja_ref, mixb_ref, mixa_ref)

    @pl.when(r == nsteps)
    def _():
        for c in range(n_out_chunks):
            out_chunk(mixb_ref if nsteps % 2 == 0 else mixa_ref, c)
        finish_half(0)
        finish_half(1)


def _prompt_layer(sinks, x, g_pre, w_in, mk, mv, w_pool, pool_scale, conv_w, w_out, g_post, l, tq, seq):
    m, d = x.shape
    nblk, nsteps, b = seq // tq, m // tq, m // seq
    nmem = mk.shape[1]
    assert nsteps >= 2
    per_seq = lambda r: (jnp.minimum(r // nblk, b - 1), 0, 0)
    prev_block = lambda r: (jnp.maximum(r - 1, 0), 0)
    return pl.pallas_call(
        functools.partial(_prompt_layer_kernel, l=l, tq=tq, nblk=nblk, nsteps=nsteps),
        grid=(nsteps + 1,),
        in_specs=[pl.BlockSpec(memory_space=pltpu.SMEM),
                  pl.BlockSpec((tq, d), prev_block),
                  pl.BlockSpec((tq, d), lambda r: (jnp.minimum(r + 1, nsteps - 1), 0)),
                  _layer_block(g_pre, l), _layer_block(w_in, l),
                  pl.BlockSpec((None, nmem, D_GROUP), per_seq),
                  pl.BlockSpec((None, nmem, D_GROUP), per_seq),
                  _layer_block(w_pool, l), _layer_block(pool_scale, l), _layer_block(conv_w, l),
                  _layer_block(w_out, l), _layer_block(g_post, l)],
        out_specs=[pl.BlockSpec((tq, d), prev_block),
                   pl.BlockSpec((None, POOL_PAD, D_GROUP), per_seq),
                   pl.BlockSpec((None, CONV_WIDTH - 1, D_GROUP), per_seq),
                   pl.BlockSpec((None, SWA_WINDOW, SWA_KV_DIM), per_seq),
                   pl.BlockSpec((None, SWA_WINDOW, SWA_KV_DIM), per_seq)],
        out_shape=[jax.ShapeDtypeStruct((m, d), F32),
                   jax.ShapeDtypeStruct((b, POOL_PAD, D_GROUP), F32),
                   jax.ShapeDtypeStruct((b, CONV_WIDTH - 1, D_GROUP), F32),
                   jax.ShapeDtypeStruct((b, SWA_WINDOW, SWA_KV_DIM), F32),
                   jax.ShapeDtypeStruct((b, SWA_WINDOW, SWA_KV_DIM), F32)],
        scratch_shapes=[pltpu.VMEM((tq, D_IN), F32),
                        pltpu.VMEM((tq, D_IN), F32),
                        pltpu.VMEM((tq, d), BF16),
                        pltpu.VMEM((tq, d), BF16),
                        pltpu.VMEM((tq, d), BF16),
                        pltpu.VMEM((tq, d), F32),
                        pltpu.VMEM((POOL_HALO + tq, D_GROUP), F32),
                        pltpu.VMEM((CONV_HALO + tq, D_GROUP), F32),
                        pltpu.VMEM((SWA_WINDOW + tq, SWA_KV_DIM), F32),
                        pltpu.VMEM((SWA_WINDOW + tq, SWA_KV_DIM), F32)],
        compiler_params=_params("arbitrary", vmem=PROMPT_LAYER_VMEM_BYTES),
        name="prompt_layer",
    )(sinks, x, x, g_pre, w_in, mk, mv, w_pool, pool_scale, conv_w, w_out, g_post)


def _sample_mixer_kernel(sinks_ref, proj_ref, pool_ref, conv_ref, kc_ref, vc_ref, mk_ref, mv_ref,
                         wpool_ref, pscale_ref, convw_ref, *refs, l, nb, t_new, pos0, creates_states):
    mix_ref = refs[-11]
    state_refs = refs[-10:-6]
    if creates_states:
        for ref in state_refs:
            for other in range(ref.shape[0]):
                if other != l:
                    ref[other] = jnp.zeros(ref.shape[1:], F32)
        state_refs = [ref.at[l] for ref in state_refs]
    npool_ref, nconv_ref, nk_ref, nv_ref = state_refs
    uext_all, cext_all, kext_all, vext_all, qs_all, qm_all = refs[-6:]
    uext_all[...] = jnp.zeros(uext_all.shape, F32)
    cext_all[...] = jnp.zeros(cext_all.shape, F32)
    kext_all[...] = jnp.zeros(kext_all.shape, F32)
    vext_all[...] = jnp.zeros(vext_all.shape, F32)

    n_swa = SWA_Q_HEADS * t_new
    n_mem = MEM_HEADS * t_new
    low = lax.broadcasted_iota(jnp.int32, (t_new, LANES), 1) < SWA_HEAD_DIM
    rows_of = lambda a: slice(a * t_new, (a + 1) * t_new)

    deltas = []
    for b in range(nb):
        r = rows_of(b)
        uext_ref, cext_ref, kext_ref, vext_ref = uext_all.at[b], cext_all.at[b], kext_all.at[b], vext_all.at[b]
        uext_ref[POOL_HALO - POOL_PAD:POOL_HALO, :] = pool_ref[b]
        deltas.append(_pool_delta(uext_ref, proj_ref[r, OFF_U:OFF_U + D_GROUP], pos0))
        npool_ref[b] = uext_ref[t_new + POOL_HALO - POOL_PAD:t_new + POOL_HALO, :]

        cext_ref[CONV_HALO - 2:CONV_HALO, :] = conv_ref[b]
        mix_ref[r, MIX_CONV:MIX_CONV + D_GROUP] = _conv_mixer(
            cext_ref, proj_ref[r, OFF_HC:OFF_HC + D_GROUP], proj_ref[r, OFF_GB:OFF_GB + D_GROUP],
            proj_ref[r, OFF_GC:OFF_GC + D_GROUP], convw_ref)
        nconv_ref[b] = cext_ref[t_new + CONV_HALO - 2:t_new + CONV_HALO, :]

        kext_ref[0:SWA_WINDOW, :] = kc_ref[b]
        vext_ref[0:SWA_WINDOW, :] = vc_ref[b]
        kext_ref[SWA_WINDOW:SWA_WINDOW + t_new, :] = proj_ref[r, OFF_K:OFF_K + SWA_KV_DIM]
        vext_ref[SWA_WINDOW:SWA_WINDOW + t_new, :] = proj_ref[r, OFF_V:OFF_V + SWA_KV_DIM]
        nk_ref[b] = kext_ref[t_new:t_new + SWA_WINDOW, :]
        nv_ref[b] = vext_ref[t_new:t_new + SWA_WINDOW, :]
        for pair in range(SWA_Q_HEADS // 2):
            qp = proj_ref[r, OFF_Q + pair * LANES:OFF_Q + (pair + 1) * LANES] * SWA_SCALE
            qr = pltpu.roll(qp, SWA_HEAD_DIM, 1)
            if pair // 2 == 0:
                q_even, q_odd = jnp.where(low, qp, 0.0), jnp.where(low, qr, 0.0)
            else:
                q_even, q_odd = jnp.where(low, 0.0, qr), jnp.where(low, 0.0, qp)
            qs_all[b, rows_of(2 * pair), :] = q_even
            qs_all[b, rows_of(2 * pair + 1), :] = q_odd
        for hm in range(MEM_HEADS):
            qm_all[b, rows_of(hm), :] = (
                proj_ref[r, OFF_QM + hm * MEM_HEAD_DIM:OFF_QM + (hm + 1) * MEM_HEAD_DIM])

    deltas = [jnp.concatenate([d[gi] for d in deltas], axis=0) for gi in range(len(POOL_WINDOWS))]
    for gi, y in enumerate(_pool_project(deltas, wpool_ref, pscale_ref)):
        mix_ref[:, MIX_POOL + gi * POOL_CH:MIX_POOL + (gi + 1) * POOL_CH] = y

    nt_dims = (((1,), (1,)), ((), ()))
    s_swa = jnp.concatenate(
        [lax.dot_general(qs_all[b].astype(BF16), kext_all[b].astype(BF16),
                         nt_dims, preferred_element_type=F32) for b in range(nb)], axis=0)
    s_mem = jnp.concatenate(
        [lax.dot_general(qm_all[b].astype(BF16), mk_ref[b].astype(BF16),
                         nt_dims, preferred_element_type=F32) for b in range(nb)], axis=0)

    ri = lax.broadcasted_iota(jnp.int32, s_swa.shape, 0)
    tok = ri % t_new
    kj = lax.broadcasted_iota(jnp.int32, s_swa.shape, 1)
    valid = (kj > tok) & (kj <= tok + SWA_WINDOW) & (kj >= max(SWA_WINDOW - pos0, 0))
    head = (lax.broadcasted_iota(jnp.int32, (s_swa.shape[0], 1), 0) // t_new) % SWA_Q_HEADS
    sink = jnp.zeros(head.shape, F32)
    for hq in range(SWA_Q_HEADS):
        sink = jnp.where(head == hq, sinks_ref[l, hq], sink)
    s_swa = jnp.where(valid, s_swa, -jnp.inf)
    m = jnp.maximum(jnp.max(s_swa, axis=1, keepdims=True), sink)
    e = jnp.exp(s_swa - m)
    p_swa = (e * (1.0 / (jnp.sum(e, axis=1, keepdims=True) + jnp.exp(sink - m)))).astype(BF16)

    mem_valid = (lax.broadcasted_iota(jnp.int32, s_mem.shape, 1) % MEM_HEADS
                 == (lax.broadcasted_iota(jnp.int32, s_mem.shape, 0) // t_new) % MEM_HEADS)
    s_mem = jnp.where(mem_valid, s_mem * MEM_SCALE, -jnp.inf)
    e = jnp.exp(s_mem - jnp.max(s_mem, axis=1, keepdims=True))
    p_mem = (e * (1.0 / jnp.sum(e, axis=1, keepdims=True))).astype(BF16)

    for b in range(nb):
        r = rows_of(b)
        o = jnp.dot(p_swa[b * n_swa:(b + 1) * n_swa, :], vext_all[b].astype(BF16), preferred_element_type=F32)
        o_roll = pltpu.roll(o, SWA_HEAD_DIM, 1)
        for pair in range(SWA_Q_HEADS // 2):
            src_even, src_odd = (o, o_roll) if pair // 2 == 0 else (o_roll, o)
            mix_ref[r, MIX_SWA + pair * LANES:MIX_SWA + (pair + 1) * LANES] = jnp.where(
                low, src_even[rows_of(2 * pair), :], src_odd[rows_of(2 * pair + 1), :])
        o = jnp.dot(p_mem[b * n_mem:(b + 1) * n_mem, :], mv_ref[b].astype(BF16), preferred_element_type=F32)
        for hm in range(MEM_HEADS):
            mix_ref[r, MIX_MEM + hm * MEM_HEAD_DIM:MIX_MEM + (hm + 1) * MEM_HEAD_DIM] = o[rows_of(hm), :]


def _sample_mixer(sinks, proj, pool, conv, kc, vc, mk, mv, w_pool, pool_scale, conv_w, new_states,
                  l, nb, t_new, pos0):
    b = proj.shape[0] // t_new
    slots = nb
    blk = lambda a: pl.BlockSpec((None, nb) + a.shape[2:], lambda i: (l, i) + (0,) * (a.ndim - 2))
    out_sds = lambda a: jax.ShapeDtypeStruct(a.shape, F32)
    n_in = 11
    carried = [] if new_states is None else list(new_states)
    if carried:
        state_blk = blk
    else:
        state_blk = lambda a: pl.BlockSpec((a.shape[0], nb) + a.shape[2:], lambda i: (0, i) + (0,) * (a.ndim - 2))
    return pl.pallas_call(
        functools.partial(_sample_mixer_kernel, l=l, nb=nb, t_new=t_new, pos0=pos0, creates_states=not carried),
        grid=(b // nb,),
        in_specs=[pl.BlockSpec(memory_space=pltpu.SMEM),
                  pl.BlockSpec((nb * t_new, D_IN), lambda i: (i, 0)),
                  blk(pool), blk(conv), blk(kc), blk(vc), blk(mk), blk(mv),
                  _layer_block(w_pool, l), _layer_block(pool_scale, l), _layer_block(conv_w, l)]
                 + [pl.BlockSpec(memory_space=pl.ANY)] * len(carried),
        input_output_aliases={n_in + k: 1 + k for k in range(len(carried))},
        out_specs=[pl.BlockSpec((nb * t_new, 4 * D_GROUP), lambda i: (i, 0)),
                   state_blk(pool), state_blk(conv), state_blk(kc), state_blk(vc)],
        out_shape=[jax.ShapeDtypeStruct((b * t_new, 4 * D_GROUP), F32),
                   out_sds(pool), out_sds(conv), out_sds(kc), out_sds(vc)],
        scratch_shapes=[pltpu.VMEM((slots, POOL_HALO + 8, D_GROUP), F32),
                        pltpu.VMEM((slots, CONV_HALO + 8, D_GROUP), F32),
                        pltpu.VMEM((slots, 2 * SWA_WINDOW, SWA_KV_DIM), F32),
                        pltpu.VMEM((slots, 2 * SWA_WINDOW, SWA_KV_DIM), F32),
                        pltpu.VMEM((slots, SWA_Q_HEADS * t_new, LANES), F32),
                        pltpu.VMEM((slots, MEM_HEADS * t_new, MEM_HEAD_DIM), F32)],
        compiler_params=_params("arbitrary"),
        name="sample_mixer",
    )(sinks, proj, pool, conv, kc, vc, mk, mv, w_pool, pool_scale, conv_w, *carried)


def kernel(x_prompt, x_sample, mem_prompt, state_pool, state_conv, cache_swa_k, cache_swa_v,
           cache_mem_k, cache_mem_v, g_mix_pre, w_in, w_pool, pool_scale, conv_w, swa_sinks,
           g_mem, w_mem_kv, w_out, g_mix_post, g_mlp_pre, w_up, w_down, g_mlp_post):
    depth = w_in.shape[0]
    bp, seq, d_model = x_prompt.shape
    bs, t_new, _ = x_sample.shape
    nmem = mem_prompt.shape[1]
    assert w_in.shape[2] == D_IN and d_model == 4 * D_GROUP

    w_in_b, w_out_b = w_in.astype(BF16), w_out.astype(BF16)
    w_pool_b = w_pool.astype(BF16)
    rows = lambda a: a.reshape(depth, 1, a.shape[-1])
    g_mix_pre, g_mem, g_mix_post = rows(g_mix_pre), rows(g_mem), rows(g_mix_post)
    g_mlp_pre, g_mlp_post, pool_scale = rows(g_mlp_pre), rows(g_mlp_post), rows(pool_scale)

    yp = x_prompt.reshape(bp * seq, d_model)
    ys = x_sample.reshape(bs * t_new, d_model)
    mem = mem_prompt.reshape(bp * nmem, d_model)
    kc = cache_swa_k.reshape(depth, bs, SWA_WINDOW, SWA_KV_DIM)
    vc = cache_swa_v.reshape(depth, bs, SWA_WINDOW, SWA_KV_DIM)
    mkc = cache_mem_k.reshape(depth, bs, nmem * MEM_HEADS, MEM_HEAD_DIM)
    mvc = cache_mem_v.reshape(depth, bs, nmem * MEM_HEADS, MEM_HEAD_DIM)

    outs = [[] for _ in range(6)]
    sample_states = None
    for l in range(depth):
        proj = _norm_matmul(ys, g_mix_pre, w_in_b, l, tm=512)
        mix, *sample_states = _sample_mixer(
            swa_sinks, proj, state_pool, state_conv, kc, vc, mkc, mvc, w_pool_b, pool_scale, conv_w,
            sample_states, l, nb=8, t_new=t_new, pos0=PAST_LEN)
        ys = _matmul_norm_res(mix, w_out_b, g_mix_post, ys, l, tm=512)
        ys, w_up_b, w_down_b = _mlp(ys, g_mlp_pre, w_up, w_down, g_mlp_post, l, tm=bs * t_new, tf=512)

        mk, mv = _norm_matmul(mem, g_mem, w_mem_kv, l, tm=512, n_out=2)
        mk, mv = mk.reshape(bp, nmem, D_GROUP), mv.reshape(bp, nmem, D_GROUP)
        yp, pool_p, conv_p, k_p, v_p = _prompt_layer(swa_sinks, yp, g_mix_pre, w_in_b, mk, mv, w_pool_b, pool_scale,
                                                     conv_w, w_out_b, g_mix_post, l, tq=256, seq=seq)
        yp = _mlp(yp, g_mlp_pre, w_up_b, w_down_b, g_mlp_post, l, tm=1024, tf=512)

        for lst, val in zip(outs, (pool_p, conv_p, k_p, v_p, mk, mv)):
            lst.append(val)

    pool_p, conv_p, k_p, v_p, mk, mv = (jnp.stack(o) for o in outs)
    pool_s, conv_s, k_s, v_s = sample_states
    kv_shape = lambda a: a.reshape(a.shape[:2] + (SWA_WINDOW, 2, SWA_HEAD_DIM))
    mem_shape = lambda a: a.reshape(depth, bp, nmem, MEM_HEADS, MEM_HEAD_DIM)
    return (yp.reshape(bp, seq, d_model), ys.reshape(bs, t_new, d_model), pool_p, pool_s, conv_p, conv_s,
            kv_shape(k_p), kv_shape(k_s), kv_shape(v_p), kv_shape(v_s), mem_shape(mk), mem_shape(mv))
```

```python
import functools
import math

import jax
import jax.numpy as jnp
from jax import lax
from jax.experimental import pallas as pl
from jax.experimental.pallas import tpu as pltpu

F32 = jnp.float32
BF16 = jnp.bfloat16

LANES = 128
D_GROUP = 512
POOL_WINDOWS = (2, 4, 8, 16)
POOL_CH = D_GROUP // len(POOL_WINDOWS)
POOL_PAD = max(POOL_WINDOWS) - 1
POOL_HALO = 16
CONV_WIDTH = 3
CONV_HALO = 8
SWA_WINDOW = 128
SWA_HEAD_DIM = 64
SWA_Q_HEADS = 8
SWA_GROUP = 4
SWA_KV_DIM = 128
SWA_SCALE = 1.0 / math.sqrt(SWA_HEAD_DIM)
MEM_HEADS = 4
MEM_HEAD_DIM = D_GROUP // MEM_HEADS
MEM_SCALE = 1.0 / math.sqrt(MEM_HEAD_DIM)
RMS_EPS = 1e-6
PAST_LEN = 8192
OFF_U, OFF_HC, OFF_GB, OFF_GC, OFF_Q = 0, 512, 1024, 1536, 2048
OFF_K, OFF_V, OFF_QM, D_IN = 2560, 2688, 2816, 3328
MIX_POOL, MIX_CONV, MIX_SWA, MIX_MEM = 0, 512, 1024, 1536

VMEM_LIMIT_BYTES = 56 * 1024 * 1024
PROMPT_LAYER_VMEM_BYTES = 60 * 1024 * 1024


def _rmsnorm(x, g):
    ms = jnp.mean(x * x, axis=-1, keepdims=True)
    return x * lax.rsqrt(ms + RMS_EPS) * g


def _params(*sem, vmem=VMEM_LIMIT_BYTES):
    return pltpu.CompilerParams(dimension_semantics=sem, vmem_limit_bytes=vmem)


def _whole_block(a):
    zeros = (0,) * a.ndim
    return pl.BlockSpec(a.shape, lambda *_: zeros, pipeline_mode=pl.Buffered(1))


def _layer_block(a, l):
    rest = (0,) * (a.ndim - 1)
    return pl.BlockSpec((None,) + a.shape[1:], lambda *_: (l,) + rest, pipeline_mode=pl.Buffered(1))


def _norm_matmul_kernel(x_ref, g_ref, w_ref, *o_refs):
    h = _rmsnorm(x_ref[...], g_ref[...]).astype(BF16)
    w = w_ref[...]
    if w.dtype != BF16:
        w = w.astype(BF16)
    y = jnp.dot(h, w, preferred_element_type=F32)
    n = y.shape[1] // len(o_refs)
    for i, o_ref in enumerate(o_refs):
        o_ref[...] = y[:, i * n:(i + 1) * n]


def _norm_matmul(x, g, w, l, tm, n_out=1):
    m, k = x.shape
    n = w.shape[2] // n_out
    outs = pl.pallas_call(
        _norm_matmul_kernel,
        grid=(m // tm,),
        in_specs=[pl.BlockSpec((tm, k), lambda i: (i, 0)), _layer_block(g, l), _layer_block(w, l)],
        out_specs=[pl.BlockSpec((tm, n), lambda i: (i, 0))] * n_out,
        out_shape=[jax.ShapeDtypeStruct((m, n), F32)] * n_out,
        compiler_params=_params("arbitrary"),
        name="norm_matmul",
    )(x, g, w)
    return outs[0] if n_out == 1 else outs


def _norm_matmul_cast_kernel(x_ref, g_ref, w_ref, o_ref, wb_ref, h_ref):
    @pl.when(pl.program_id(0) == 0)
    def _():
        h_ref[...] = _rmsnorm(x_ref[...], g_ref[...]).astype(BF16)

    w = w_ref[...].astype(BF16)
    wb_ref[...] = w
    o_ref[...] = jnp.dot(h_ref[...], w, preferred_element_type=F32)


def _norm_matmul_cast(x, g, w, l, tn):
    m, k = x.shape
    n = w.shape[2]
    return pl.pallas_call(
        _norm_matmul_cast_kernel,
        grid=(n // tn,),
        in_specs=[pl.BlockSpec((m, k), lambda j: (0, 0)), _layer_block(g, l),
                  pl.BlockSpec((None, k, tn), lambda j: (l, 0, j))],
        out_specs=[pl.BlockSpec((m, tn), lambda j: (0, j)), pl.BlockSpec((k, tn), lambda j: (0, j))],
        out_shape=[jax.ShapeDtypeStruct((m, n), F32), jax.ShapeDtypeStruct((k, n), BF16)],
        scratch_shapes=[pltpu.VMEM((m, k), BF16)],
        compiler_params=_params("arbitrary"),
        name="norm_matmul_cast",
    )(x, g, w)


def _matmul_norm_res_cast_kernel(a_ref, w_ref, g_ref, x_ref, o_ref, wb_ref):
    kk = pl.program_id(0)
    w = w_ref[...].astype(BF16)
    wb_ref[...] = w
    part = jnp.dot(a_ref[...].astype(BF16), w, preferred_element_type=F32)

    @pl.when(kk == 0)
    def _():
        o_ref[...] = part

    @pl.when(kk > 0)
    def _():
        o_ref[...] += part

    @pl.when(kk == pl.num_programs(0) - 1)
    def _():
        o_ref[...] = x_ref[...] + _rmsnorm(o_ref[...], g_ref[...])


def _matmul_norm_res_cast(a, w, g, x, l, tk):
    m, k = a.shape
    n = w.shape[2]
    return pl.pallas_call(
        _matmul_norm_res_cast_kernel,
        grid=(k // tk,),
        in_specs=[pl.BlockSpec((m, tk), lambda kk: (0, kk)), pl.BlockSpec((None, tk, n), lambda kk: (l, kk, 0)),
                  _layer_block(g, l), pl.BlockSpec((m, n), lambda kk: (0, 0))],
        out_specs=[pl.BlockSpec((m, n), lambda kk: (0, 0)), pl.BlockSpec((tk, n), lambda kk: (kk, 0))],
        out_shape=[jax.ShapeDtypeStruct((m, n), F32), jax.ShapeDtypeStruct((k, n), BF16)],
        compiler_params=_params("arbitrary"),
        name="matmul_norm_res_cast",
    )(a, w, g, x)


def _mlp_kernel(x_ref, gpre_ref, wup_ref, wdown_ref, gpost_ref, o_ref, *rest, cast_weights):
    f = pl.program_id(1)
    h_ref = rest[-1]

    @pl.when(f == 0)
    def _():
        h_ref[...] = _rmsnorm(x_ref[...], gpre_ref[...]).astype(BF16)
        o_ref[...] = jnp.zeros(o_ref.shape, F32)

    if cast_weights:
        wup_b_ref, wdown_b_ref = rest[:2]
        wup, wdown = wup_ref[...].astype(BF16), wdown_ref[...].astype(BF16)
        wup_b_ref[...] = wup
        wdown_b_ref[...] = wdown
    else:
        wup, wdown = wup_ref[...], wdown_ref[...]
    a = jnp.dot(h_ref[...], wup, preferred_element_type=F32)
    a = jnp.square(jnp.maximum(a, 0.0)).astype(BF16)
    o_ref[...] += jnp.dot(a, wdown, preferred_element_type=F32)

    @pl.when(f == pl.num_programs(1) - 1)
    def _():
        o_ref[...] = x_ref[...] + _rmsnorm(o_ref[...], gpost_ref[...])


def _mlp(x, g_pre, w_up, w_down, g_post, l, tm, tf):
    m, d = x.shape
    cast_weights = w_up.ndim == 3
    ff = w_up.shape[-1]
    if cast_weights:
        assert m == tm, "each weight block must be visited exactly once"
        w_specs = [pl.BlockSpec((None, d, tf), lambda i, f: (l, 0, f)),
                   pl.BlockSpec((None, tf, d), lambda i, f: (l, f, 0))]
    else:
        w_specs = [pl.BlockSpec((d, tf), lambda i, f: (0, f)), pl.BlockSpec((tf, d), lambda i, f: (f, 0))]
    out_specs = [pl.BlockSpec((tm, d), lambda i, f: (i, 0))]
    out_shape = [jax.ShapeDtypeStruct((m, d), F32)]
    if cast_weights:
        out_specs += [pl.BlockSpec((d, tf), lambda i, f: (0, f)), pl.BlockSpec((tf, d), lambda i, f: (f, 0))]
        out_shape += [jax.ShapeDtypeStruct((d, ff), BF16), jax.ShapeDtypeStruct((ff, d), BF16)]
    outs = pl.pallas_call(
        functools.partial(_mlp_kernel, cast_weights=cast_weights),
        grid=(m // tm, ff // tf),
        in_specs=[pl.BlockSpec((tm, d), lambda i, f: (i, 0)), _layer_block(g_pre, l)] + w_specs
                 + [_layer_block(g_post, l)],
        out_specs=out_specs,
        out_shape=out_shape,
        scratch_shapes=[pltpu.VMEM((tm, d), BF16)],
        compiler_params=_params("arbitrary", "arbitrary"),
        name="mlp",
    )(x, g_pre, w_up, w_down, g_post)
    return outs if cast_weights else outs[0]


def _pool_delta(uext_ref, u, pos0):
    rows = u.shape[0]
    uext_ref[POOL_HALO:POOL_HALO + rows, :] = u
    return [_pool_group_delta(uext_ref, rows, pos0, gi) for gi in range(len(POOL_WINDOWS))]


def _pool_group_delta(uext_ref, rows, pos0, gi):
    w = POOL_WINDOWS[gi]
    cols = slice(gi * POOL_CH, (gi + 1) * POOL_CH)
    u = uext_ref[POOL_HALO:POOL_HALO + rows, cols]
    s = u
    for back in range(1, w):
        s = s + uext_ref[POOL_HALO - back:POOL_HALO - back + rows, cols]
    pos = pos0 + lax.broadcasted_iota(jnp.int32, (rows, POOL_CH), 0)
    cnt = jnp.minimum(pos + 1, w).astype(F32)
    return s / cnt - u


def _pool_group_project(gi, d, wpool_ref, pscale_ref):
    y = jnp.dot(d.astype(BF16), wpool_ref[gi], preferred_element_type=F32)
    return y * pscale_ref[:, gi * POOL_CH:(gi + 1) * POOL_CH]


def _pool_project(deltas, wpool_ref, pscale_ref):
    return [_pool_group_project(gi, d, wpool_ref, pscale_ref) for gi, d in enumerate(deltas)]


def _conv_mixer(cext_ref, hc, gb, gc, convw_ref):
    rows = hc.shape[0]
    cext_ref[CONV_HALO:CONV_HALO + rows, :] = gc * hc
    conv = convw_ref[0:1, :] * cext_ref[CONV_HALO - 2:CONV_HALO - 2 + rows, :]
    for kk in range(1, CONV_WIDTH):
        lo = CONV_HALO - 2 + kk
        conv = conv + convw_ref[kk:kk + 1, :] * cext_ref[lo:lo + rows, :]
    return gb * conv


def _swa_kv_head(h, q_of_pair, kwin, vwin, sinks_ref, l, first_key):
    nk = kwin.shape[0]
    low = lax.broadcasted_iota(jnp.int32, (nk, LANES), 1) < SWA_HEAD_DIM
    kroll = pltpu.roll(kwin, SWA_HEAD_DIM, 1)
    vroll = pltpu.roll(vwin, SWA_HEAD_DIM, 1)
    ind_a = jnp.where(low, 1.0, 0.0)
    ind_b = 1.0 - ind_a
    if h == 0:
        k_a, k_b = jnp.where(low, kwin, 0.0), jnp.where(low, 0.0, kroll)
        v_a, v_b = jnp.where(low, vwin, 0.0), jnp.where(low, 0.0, vroll)
    else:
        k_a, k_b = jnp.where(low, kroll, 0.0), jnp.where(low, 0.0, kwin)
        v_a, v_b = jnp.where(low, vroll, 0.0), jnp.where(low, 0.0, vwin)
    kst = jnp.concatenate([k_a, k_b], axis=0).astype(BF16)
    vst = jnp.concatenate([jnp.concatenate([v_a, ind_a], axis=1),
                           jnp.concatenate([v_b, ind_b], axis=1)], axis=0).astype(BF16)
    outs = {}
    for r in range(2):
        pair = 2 * h + r
        q = q_of_pair(pair)
        rows = q.shape[0]
        qi = lax.broadcasted_iota(jnp.int32, (rows, 2 * nk), 0)
        kj = lax.broadcasted_iota(jnp.int32, (rows, 2 * nk), 1) & (nk - 1)
        valid = (kj > qi) & (kj <= qi + SWA_WINDOW) & (kj >= first_key)
        s = lax.dot_general((q * SWA_SCALE).astype(BF16), kst, (((1,), (1,)), ((), ())),
                            preferred_element_type=F32)
        s = jnp.where(valid, s, -jnp.inf)
        sink_a = sinks_ref[l, 4 * h + 2 * r]
        sink_b = sinks_ref[l, 4 * h + 2 * r + 1]
        m_a = jnp.maximum(jnp.max(s[:, :nk], axis=1, keepdims=True), sink_a)
        m_b = jnp.maximum(jnp.max(s[:, nk:], axis=1, keepdims=True), sink_b)
        p = jnp.concatenate([jnp.exp(s[:, :nk] - m_a), jnp.exp(s[:, nk:] - m_b)],
                            axis=1).astype(BF16)
        o = jnp.dot(p, vst, preferred_element_type=F32)
        low_q = lax.broadcasted_iota(jnp.int32, (rows, LANES), 1) < SWA_HEAD_DIM
        den = o[:, LANES:] + jnp.where(low_q, jnp.exp(sink_a - m_a), jnp.exp(sink_b - m_b))
        outs[pair] = o[:, :LANES] / den
    return outs


def _mem_head(qm_h, mk_h, mv_h):
    s = lax.dot_general(qm_h.astype(BF16), mk_h, (((1,), (1,)), ((), ())),
                        preferred_element_type=F32) * MEM_SCALE
    p = jnp.exp(s - jnp.max(s, axis=1, keepdims=True)).astype(BF16)
    vext = jnp.concatenate([mv_h, jnp.ones(mv_h.shape, BF16)], axis=1)
    o = jnp.dot(p, vext, preferred_element_type=F32)
    return o[:, :MEM_HEAD_DIM] / o[:, MEM_HEAD_DIM:]


PROJ_CHUNK = 512
N_PROJ_CHUNKS = -(-D_IN // PROJ_CHUNK)


def _prompt_layer_kernel(sinks_ref, xp_ref, xn_ref, gpre_ref, win_ref, mk_ref, mv_ref, wpool_ref, pscale_ref,
                         convw_ref, wout_ref, gpost_ref,
                         o_ref, npool_ref, nconv_ref, nk_ref, nv_ref,
                         proja_ref, projb_ref, h_ref, mixa_ref, mixb_ref, y_ref, uext_ref, cext_ref, kext_ref,
                         vext_ref, *, l, tq, nblk, nsteps):
    r = pl.program_id(0)
    j = r % nblk
    half = tq // 2

    def project_chunk(dst_ref, c):
        cols = slice(c * PROJ_CHUNK, min((c + 1) * PROJ_CHUNK, D_IN))
        dst_ref[:, cols] = jnp.dot(h_ref[...], win_ref[:, cols], preferred_element_type=F32)

    n_out_chunks = wout_ref.shape[1] // PROJ_CHUNK
    n_norm_parts = 4

    def out_chunk(m_ref, c):
        cols = slice(c * PROJ_CHUNK, (c + 1) * PROJ_CHUNK)
        y_ref[:, cols] = jnp.dot(m_ref[...], wout_ref[:, cols], preferred_element_type=F32)

    def finish_half(k):
        rows = slice(k * half, (k + 1) * half)
        o_ref[rows, :] = xp_ref[rows, :] + _rmsnorm(y_ref[rows, :], gpost_ref[...])

    def norm_part(k):
        rows = slice(k * tq // n_norm_parts, (k + 1) * tq // n_norm_parts)
        h_ref[rows, :] = _rmsnorm(xn_ref[rows, :], gpre_ref[...]).astype(BF16)

    @pl.when(r == 0)
    def _():
        h_ref[...] = _rmsnorm(xp_ref[...], gpre_ref[...]).astype(BF16)
        for c in range(N_PROJ_CHUNKS):
            project_chunk(proja_ref, c)
        mixb_ref[...] = jnp.zeros(mixb_ref.shape, BF16)

    @pl.when(j == 0)
    def _():
        uext_ref[0:POOL_HALO, :] = jnp.zeros((POOL_HALO, D_GROUP), F32)
        cext_ref[0:CONV_HALO, :] = jnp.zeros((CONV_HALO, D_GROUP), F32)
        kext_ref[0:SWA_WINDOW, :] = jnp.zeros((SWA_WINDOW, SWA_KV_DIM), F32)
        vext_ref[0:SWA_WINDOW, :] = jnp.zeros((SWA_WINDOW, SWA_KV_DIM), F32)

    def mix_block(p_ref, pn_ref, m_ref, mp_ref):
        proj_chunks = [functools.partial(project_chunk, pn_ref, c) for c in range(N_PROJ_CHUNKS)]

        def put(c0, y, rows=slice(None)):
            m_ref[rows, c0:c0 + y.shape[1]] = y.astype(BF16)

        def pool_fill_piece():
            uext_ref[POOL_HALO:POOL_HALO + tq, :] = p_ref[:, OFF_U:OFF_U + D_GROUP]
            npool_ref[...] = uext_ref[tq + POOL_HALO - POOL_PAD:tq + POOL_HALO, :]

        def pool_group_piece(gi):
            d = _pool_group_delta(uext_ref, tq, j * tq, gi)
            put(MIX_POOL + gi * POOL_CH, _pool_group_project(gi, d, wpool_ref, pscale_ref))

        def pool_carry_piece():
            uext_ref[0:POOL_HALO, :] = uext_ref[tq:tq + POOL_HALO, :]

        def conv_piece():
            y = _conv_mixer(cext_ref, p_ref[:, OFF_HC:OFF_HC + D_GROUP], p_ref[:, OFF_GB:OFF_GB + D_GROUP],
                            p_ref[:, OFF_GC:OFF_GC + D_GROUP], convw_ref)
            put(MIX_CONV, y)
            nconv_ref[...] = cext_ref[tq + CONV_HALO - 2:tq + CONV_HALO, :]
            cext_ref[0:CONV_HALO, :] = cext_ref[tq:tq + CONV_HALO, :]

        def kv_fill_piece():
            kext_ref[SWA_WINDOW:SWA_WINDOW + tq, :] = p_ref[:, OFF_K:OFF_K + SWA_KV_DIM]
            vext_ref[SWA_WINDOW:SWA_WINDOW + tq, :] = p_ref[:, OFF_V:OFF_V + SWA_KV_DIM]
            nk_ref[...] = kext_ref[tq:tq + SWA_WINDOW, :]
            nv_ref[...] = vext_ref[tq:tq + SWA_WINDOW, :]

        def swa_piece(sb, h):
            r0 = sb * SWA_WINDOW
            rows = slice(r0, r0 + SWA_WINDOW)
            first_key = jnp.maximum(SWA_WINDOW - (j * tq + r0), 0)
            q_of_pair = lambda pair: p_ref[rows, OFF_Q + pair * LANES:OFF_Q + (pair + 1) * LANES]
            outs = _swa_kv_head(h, q_of_pair, kext_ref[r0:r0 + 2 * SWA_WINDOW, :],
                                vext_ref[r0:r0 + 2 * SWA_WINDOW, :], sinks_ref, l, first_key)
            for pair, y in outs.items():
                put(MIX_SWA + pair * LANES, y, rows)

        def kv_carry_piece():
            kext_ref[0:SWA_WINDOW, :] = kext_ref[tq:tq + SWA_WINDOW, :]
            vext_ref[0:SWA_WINDOW, :] = vext_ref[tq:tq + SWA_WINDOW, :]

        def mem_piece(hm):
            c0 = hm * MEM_HEAD_DIM
            y = _mem_head(p_ref[:, OFF_QM + c0:OFF_QM + c0 + MEM_HEAD_DIM],
                          mk_ref[:, c0:c0 + MEM_HEAD_DIM].astype(BF16), mv_ref[:, c0:c0 + MEM_HEAD_DIM].astype(BF16))
            put(MIX_MEM + c0, y)

        mxu_items = [functools.partial(out_chunk, mp_ref, c) for c in range(n_out_chunks)] + proj_chunks
        vpu_items = [functools.partial(norm_part, k) for k in range(n_norm_parts)]
        vpu_items += [pool_fill_piece] + [functools.partial(pool_group_piece, gi) for gi in range(len(POOL_WINDOWS))]
        vpu_items += [pool_carry_piece, functools.partial(finish_half, 0), conv_piece,
                      functools.partial(finish_half, 1), kv_fill_piece]
        vpu_items += [functools.partial(swa_piece, sb, h) for sb in range(tq // SWA_WINDOW) for h in range(2)]
        vpu_items += [kv_carry_piece] + [functools.partial(mem_piece, hm) for hm in range(MEM_HEADS)]
        mxu_items[0]()
        rest = mxu_items[1:]
        for i, item in enumerate(vpu_items):
            item()
            for c in range(i * len(rest) // len(vpu_items), (i + 1) * len(rest) // len(vpu_items)):
                rest[c]()

    @pl.when((r < nsteps) & (r % 2 == 0))
    def _():
        mix_block(proja_ref, projb_ref, mixa_ref, mixb_ref)

    @pl.when((r < nsteps) & (r % 2 == 1))
    def _():
        mix_block(projb_ref, proja_ref, mixb_ref, mixa_ref)

    @pl.when(r == nsteps)
    def _():
        for c in range(n_out_chunks):
            out_chunk(mixb_ref if nsteps % 2 == 0 else mixa_ref, c)
        finish_half(0)
        finish_half(1)


def _prompt_layer(sinks, x, g_pre, w_in, mk, mv, w_pool, pool_scale, conv_w, w_out, g_post, l, tq, seq):
    m, d = x.shape
    nblk, nsteps, b = seq // tq, m // tq, m // seq
    nmem = mk.shape[1]
    assert nsteps >= 2
    per_seq = lambda r: (jnp.minimum(r // nblk, b - 1), 0, 0)
    prev_block = lambda r: (jnp.maximum(r - 1, 0), 0)
    return pl.pallas_call(
        functools.partial(_prompt_layer_kernel, l=l, tq=tq, nblk=nblk, nsteps=nsteps),
        grid=(nsteps + 1,),
        in_specs=[pl.BlockSpec(memory_space=pltpu.SMEM),
                  pl.BlockSpec((tq, d), prev_block),
                  pl.BlockSpec((tq, d), lambda r: (jnp.minimum(r + 1, nsteps - 1), 0)),
                  _layer_block(g_pre, l), _whole_block(w_in),
                  pl.BlockSpec((None, nmem, D_GROUP), per_seq),
                  pl.BlockSpec((None, nmem, D_GROUP), per_seq),
                  _layer_block(w_pool, l), _layer_block(pool_scale, l), _layer_block(conv_w, l),
                  _whole_block(w_out), _layer_block(g_post, l)],
        out_specs=[pl.BlockSpec((tq, d), prev_block),
                   pl.BlockSpec((None, POOL_PAD, D_GROUP), per_seq),
                   pl.BlockSpec((None, CONV_WIDTH - 1, D_GROUP), per_seq),
                   pl.BlockSpec((None, SWA_WINDOW, SWA_KV_DIM), per_seq),
                   pl.BlockSpec((None, SWA_WINDOW, SWA_KV_DIM), per_seq)],
        out_shape=[jax.ShapeDtypeStruct((m, d), F32),
                   jax.ShapeDtypeStruct((b, POOL_PAD, D_GROUP), F32),
                   jax.ShapeDtypeStruct((b, CONV_WIDTH - 1, D_GROUP), F32),
                   jax.ShapeDtypeStruct((b, SWA_WINDOW, SWA_KV_DIM), F32),
                   jax.ShapeDtypeStruct((b, SWA_WINDOW, SWA_KV_DIM), F32)],
        scratch_shapes=[pltpu.VMEM((tq, D_IN), F32),
                        pltpu.VMEM((tq, D_IN), F32),
                        pltpu.VMEM((tq, d), BF16),
                        pltpu.VMEM((tq, d), BF16),
                        pltpu.VMEM((tq, d), BF16),
                        pltpu.VMEM((tq, d), F32),
                        pltpu.VMEM((POOL_HALO + tq, D_GROUP), F32),
                        pltpu.VMEM((CONV_HALO + tq, D_GROUP), F32),
                        pltpu.VMEM((SWA_WINDOW + tq, SWA_KV_DIM), F32),
                        pltpu.VMEM((SWA_WINDOW + tq, SWA_KV_DIM), F32)],
        compiler_params=_params("arbitrary", vmem=PROMPT_LAYER_VMEM_BYTES),
        name="prompt_layer",
    )(sinks, x, x, g_pre, w_in, mk, mv, w_pool, pool_scale, conv_w, w_out, g_post)


def _sample_mixer_kernel(sinks_ref, proj_ref, pool_ref, conv_ref, kc_ref, vc_ref, mk_ref, mv_ref,
                         wpool_ref, pscale_ref, convw_ref, *refs, l, nb, t_new, pos0, creates_states):
    mix_ref = refs[-11]
    state_refs = refs[-10:-6]
    if creates_states:
        for ref in state_refs:
            for other in range(ref.shape[0]):
                if other != l:
                    ref[other] = jnp.zeros(ref.shape[1:], F32)
        state_refs = [ref.at[l] for ref in state_refs]
    npool_ref, nconv_ref, nk_ref, nv_ref = state_refs
    uext_all, cext_all, kext_all, vext_all, qs_all, qm_all = refs[-6:]
    uext_all[...] = jnp.zeros(uext_all.shape, F32)
    cext_all[...] = jnp.zeros(cext_all.shape, F32)
    kext_all[...] = jnp.zeros(kext_all.shape, F32)
    vext_all[...] = jnp.zeros(vext_all.shape, F32)

    n_swa = SWA_Q_HEADS * t_new
    n_mem = MEM_HEADS * t_new
    low = lax.broadcasted_iota(jnp.int32, (t_new, LANES), 1) < SWA_HEAD_DIM
    rows_of = lambda a: slice(a * t_new, (a + 1) * t_new)

    deltas = []
    for b in range(nb):
        r = rows_of(b)
        uext_ref, cext_ref, kext_ref, vext_ref = uext_all.at[b], cext_all.at[b], kext_all.at[b], vext_all.at[b]
        uext_ref[POOL_HALO - POOL_PAD:POOL_HALO, :] = pool_ref[b]
        deltas.append(_pool_delta(uext_ref, proj_ref[r, OFF_U:OFF_U + D_GROUP], pos0))
        npool_ref[b] = uext_ref[t_new + POOL_HALO - POOL_PAD:t_new + POOL_HALO, :]

        cext_ref[CONV_HALO - 2:CONV_HALO, :] = conv_ref[b]
        mix_ref[r, MIX_CONV:MIX_CONV + D_GROUP] = _conv_mixer(
            cext_ref, proj_ref[r, OFF_HC:OFF_HC + D_GROUP], proj_ref[r, OFF_GB:OFF_GB + D_GROUP],
            proj_ref[r, OFF_GC:OFF_GC + D_GROUP], convw_ref)
        nconv_ref[b] = cext_ref[t_new + CONV_HALO - 2:t_new + CONV_HALO, :]

        kext_ref[0:SWA_WINDOW, :] = kc_ref[b]
        vext_ref[0:SWA_WINDOW, :] = vc_ref[b]
        kext_ref[SWA_WINDOW:SWA_WINDOW + t_new, :] = proj_ref[r, OFF_K:OFF_K + SWA_KV_DIM]
        vext_ref[SWA_WINDOW:SWA_WINDOW + t_new, :] = proj_ref[r, OFF_V:OFF_V + SWA_KV_DIM]
        nk_ref[b] = kext_ref[t_new:t_new + SWA_WINDOW, :]
        nv_ref[b] = vext_ref[t_new:t_new + SWA_WINDOW, :]
        for pair in range(SWA_Q_HEADS // 2):
            qp = proj_ref[r, OFF_Q + pair * LANES:OFF_Q + (pair + 1) * LANES] * SWA_SCALE
            qr = pltpu.roll(qp, SWA_HEAD_DIM, 1)
            if pair // 2 == 0:
                q_even, q_odd = jnp.where(low, qp, 0.0), jnp.where(low, qr, 0.0)
            else:
                q_even, q_odd = jnp.where(low, 0.0, qr), jnp.where(low, 0.0, qp)
            qs_all[b, rows_of(2 * pair), :] = q_even
            qs_all[b, rows_of(2 * pair + 1), :] = q_odd
        for hm in range(MEM_HEADS):
            qm_all[b, rows_of(hm), :] = (
                proj_ref[r, OFF_QM + hm * MEM_HEAD_DIM:OFF_QM + (hm + 1) * MEM_HEAD_DIM])

    deltas = [jnp.concatenate([d[gi] for d in deltas], axis=0) for gi in range(len(POOL_WINDOWS))]
    for gi, y in enumerate(_pool_project(deltas, wpool_ref, pscale_ref)):
        mix_ref[:, MIX_POOL + gi * POOL_CH:MIX_POOL + (gi + 1) * POOL_CH] = y

    nt_dims = (((1,), (1,)), ((), ()))
    s_swa = jnp.concatenate(
        [lax.dot_general(qs_all[b].astype(BF16), kext_all[b].astype(BF16),
                         nt_dims, preferred_element_type=F32) for b in range(nb)], axis=0)
    s_mem = jnp.concatenate(
        [lax.dot_general(qm_all[b].astype(BF16), mk_ref[b].astype(BF16),
                         nt_dims, preferred_element_type=F32) for b in range(nb)], axis=0)

    ri = lax.broadcasted_iota(jnp.int32, s_swa.shape, 0)
    tok = ri % t_new
    kj = lax.broadcasted_iota(jnp.int32, s_swa.shape, 1)
    valid = (kj > tok) & (kj <= tok + SWA_WINDOW) & (kj >= max(SWA_WINDOW - pos0, 0))
    head = (lax.broadcasted_iota(jnp.int32, (s_swa.shape[0], 1), 0) // t_new) % SWA_Q_HEADS
    sink = jnp.zeros(head.shape, F32)
    for hq in range(SWA_Q_HEADS):
        sink = jnp.where(head == hq, sinks_ref[l, hq], sink)
    s_swa = jnp.where(valid, s_swa, -jnp.inf)
    m = jnp.maximum(jnp.max(s_swa, axis=1, keepdims=True), sink)
    e = jnp.exp(s_swa - m)
    p_swa = (e * (1.0 / (jnp.sum(e, axis=1, keepdims=True) + jnp.exp(sink - m)))).astype(BF16)

    mem_valid = (lax.broadcasted_iota(jnp.int32, s_mem.shape, 1) % MEM_HEADS
                 == (lax.broadcasted_iota(jnp.int32, s_mem.shape, 0) // t_new) % MEM_HEADS)
    s_mem = jnp.where(mem_valid, s_mem * MEM_SCALE, -jnp.inf)
    e = jnp.exp(s_mem - jnp.max(s_mem, axis=1, keepdims=True))
    p_mem = (e * (1.0 / jnp.sum(e, axis=1, keepdims=True))).astype(BF16)

    for b in range(nb):
        r = rows_of(b)
        o = jnp.dot(p_swa[b * n_swa:(b + 1) * n_swa, :], vext_all[b].astype(BF16), preferred_element_type=F32)
        o_roll = pltpu.roll(o, SWA_HEAD_DIM, 1)
        for pair in range(SWA_Q_HEADS // 2):
            src_even, src_odd = (o, o_roll) if pair // 2 == 0 else (o_roll, o)
            mix_ref[r, MIX_SWA + pair * LANES:MIX_SWA + (pair + 1) * LANES] = jnp.where(
                low, src_even[rows_of(2 * pair), :], src_odd[rows_of(2 * pair + 1), :])
        o = jnp.dot(p_mem[b * n_mem:(b + 1) * n_mem, :], mv_ref[b].astype(BF16), preferred_element_type=F32)
        for hm in range(MEM_HEADS):
            mix_ref[r, MIX_MEM + hm * MEM_HEAD_DIM:MIX_MEM + (hm + 1) * MEM_HEAD_DIM] = o[rows_of(hm), :]


def _sample_mixer(sinks, proj, pool, conv, kc, vc, mk, mv, w_pool, pool_scale, conv_w, new_states,
                  l, nb, t_new, pos0):
    b = proj.shape[0] // t_new
    slots = nb
    blk = lambda a: pl.BlockSpec((None, nb) + a.shape[2:], lambda i: (l, i) + (0,) * (a.ndim - 2))
    out_sds = lambda a: jax.ShapeDtypeStruct(a.shape, F32)
    n_in = 11
    carried = [] if new_states is None else list(new_states)
    if carried:
        state_blk = blk
    else:
        state_blk = lambda a: pl.BlockSpec((a.shape[0], nb) + a.shape[2:], lambda i: (0, i) + (0,) * (a.ndim - 2))
    return pl.pallas_call(
        functools.partial(_sample_mixer_kernel, l=l, nb=nb, t_new=t_new, pos0=pos0, creates_states=not carried),
        grid=(b // nb,),
        in_specs=[pl.BlockSpec(memory_space=pltpu.SMEM),
                  pl.BlockSpec((nb * t_new, D_IN), lambda i: (i, 0)),
                  blk(pool), blk(conv), blk(kc), blk(vc), blk(mk), blk(mv),
                  _layer_block(w_pool, l), _layer_block(pool_scale, l), _layer_block(conv_w, l)]
                 + [pl.BlockSpec(memory_space=pl.ANY)] * len(carried),
        input_output_aliases={n_in + k: 1 + k for k in range(len(carried))},
        out_specs=[pl.BlockSpec((nb * t_new, 4 * D_GROUP), lambda i: (i, 0)),
                   state_blk(pool), state_blk(conv), state_blk(kc), state_blk(vc)],
        out_shape=[jax.ShapeDtypeStruct((b * t_new, 4 * D_GROUP), F32),
                   out_sds(pool), out_sds(conv), out_sds(kc), out_sds(vc)],
        scratch_shapes=[pltpu.VMEM((slots, POOL_HALO + 8, D_GROUP), F32),
                        pltpu.VMEM((slots, CONV_HALO + 8, D_GROUP), F32),
                        pltpu.VMEM((slots, 2 * SWA_WINDOW, SWA_KV_DIM), F32),
                        pltpu.VMEM((slots, 2 * SWA_WINDOW, SWA_KV_DIM), F32),
                        pltpu.VMEM((slots, SWA_Q_HEADS * t_new, LANES), F32),
                        pltpu.VMEM((slots, MEM_HEADS * t_new, MEM_HEAD_DIM), F32)],
        compiler_params=_params("arbitrary"),
        name="sample_mixer",
    )(sinks, proj, pool, conv, kc, vc, mk, mv, w_pool, pool_scale, conv_w, *carried)


def kernel(x_prompt, x_sample, mem_prompt, state_pool, state_conv, cache_swa_k, cache_swa_v,
           cache_mem_k, cache_mem_v, g_mix_pre, w_in, w_pool, pool_scale, conv_w, swa_sinks,
           g_mem, w_mem_kv, w_out, g_mix_post, g_mlp_pre, w_up, w_down, g_mlp_post):
    depth = w_in.shape[0]
    bp, seq, d_model = x_prompt.shape
    bs, t_new, _ = x_sample.shape
    nmem = mem_prompt.shape[1]
    assert w_in.shape[2] == D_IN and d_model == 4 * D_GROUP

    w_pool_b = w_pool.astype(BF16)
    rows = lambda a: a.reshape(depth, 1, a.shape[-1])
    g_mix_pre, g_mem, g_mix_post = rows(g_mix_pre), rows(g_mem), rows(g_mix_post)
    g_mlp_pre, g_mlp_post, pool_scale = rows(g_mlp_pre), rows(g_mlp_post), rows(pool_scale)

    yp = x_prompt.reshape(bp * seq, d_model)
    ys = x_sample.reshape(bs * t_new, d_model)
    mem = mem_prompt.reshape(bp * nmem, d_model)
    kc = cache_swa_k.reshape(depth, bs, SWA_WINDOW, SWA_KV_DIM)
    vc = cache_swa_v.reshape(depth, bs, SWA_WINDOW, SWA_KV_DIM)
    mkc = cache_mem_k.reshape(depth, bs, nmem * MEM_HEADS, MEM_HEAD_DIM)
    mvc = cache_mem_v.reshape(depth, bs, nmem * MEM_HEADS, MEM_HEAD_DIM)

    outs = [[] for _ in range(6)]
    sample_states = None
    for l in range(depth):
        proj, w_in_b = _norm_matmul_cast(ys, g_mix_pre, w_in, l, tn=256)
        mix, *sample_states = _sample_mixer(
            swa_sinks, proj, state_pool, state_conv, kc, vc, mkc, mvc, w_pool_b, pool_scale, conv_w,
            sample_states, l, nb=8, t_new=t_new, pos0=PAST_LEN)
        ys, w_out_b = _matmul_norm_res_cast(mix, w_out, g_mix_post, ys, l, tk=512)
        ys, w_up_b, w_down_b = _mlp(ys, g_mlp_pre, w_up, w_down, g_mlp_post, l, tm=bs * t_new, tf=512)

        mk, mv = _norm_matmul(mem, g_mem, w_mem_kv, l, tm=512, n_out=2)
        mk, mv = mk.reshape(bp, nmem, D_GROUP), mv.reshape(bp, nmem, D_GROUP)
        yp, pool_p, conv_p, k_p, v_p = _prompt_layer(swa_sinks, yp, g_mix_pre, w_in_b, mk, mv, w_pool_b, pool_scale,
                                                     conv_w, w_out_b, g_mix_post, l, tq=256, seq=seq)
        yp = _mlp(yp, g_mlp_pre, w_up_b, w_down_b, g_mlp_post, l, tm=1024, tf=512)

        for lst, val in zip(outs, (pool_p, conv_p, k_p, v_p, mk, mv)):
            lst.append(val)

    pool_p, conv_p, k_p, v_p, mk, mv = (jnp.stack(o) for o in outs)
    pool_s, conv_s, k_s, v_s = sample_states
    kv_shape = lambda a: a.reshape(a.shape[:2] + (SWA_WINDOW, 2, SWA_HEAD_DIM))
    mem_shape = lambda a: a.reshape(depth, bp, nmem, MEM_HEADS, MEM_HEAD_DIM)
    return (yp.reshape(bp, seq, d_model), ys.reshape(bs, t_new, d_model), pool_p, pool_s, conv_p, conv_s,
            kv_shape(k_p), kv_shape(k_s), kv_shape(v_p), kv_shape(v_s), mem_shape(mk), mem_shape(mv))
```

```python
import functools
import math

import jax
import jax.numpy as jnp
from jax import lax
from jax.experimental import pallas as pl
from jax.experimental.pallas import tpu as pltpu

F32 = jnp.float32
BF16 = jnp.bfloat16

LANES = 128
D_GROUP = 512
POOL_WINDOWS = (2, 4, 8, 16)
POOL_CH = D_GROUP // len(POOL_WINDOWS)
POOL_PAD = max(POOL_WINDOWS) - 1
POOL_HALO = 16
CONV_WIDTH = 3
CONV_HALO = 8
SWA_WINDOW = 128
SWA_HEAD_DIM = 64
SWA_Q_HEADS = 8
SWA_GROUP = 4
SWA_KV_DIM = 128
SWA_SCALE = 1.0 / math.sqrt(SWA_HEAD_DIM)
MEM_HEADS = 4
MEM_HEAD_DIM = D_GROUP // MEM_HEADS
MEM_SCALE = 1.0 / math.sqrt(MEM_HEAD_DIM)
RMS_EPS = 1e-6
PAST_LEN = 8192
OFF_U, OFF_HC, OFF_GB, OFF_GC, OFF_Q = 0, 512, 1024, 1536, 2048
OFF_K, OFF_V, OFF_QM, D_IN = 2560, 2688, 2816, 3328
MIX_POOL, MIX_CONV, MIX_SWA, MIX_MEM = 0, 512, 1024, 1536

VMEM_LIMIT_BYTES = 56 * 1024 * 1024
PROMPT_LAYER_VMEM_BYTES = 60 * 1024 * 1024


def _rmsnorm(x, g):
    ms = jnp.mean(x * x, axis=-1, keepdims=True)
    return x * lax.rsqrt(ms + RMS_EPS) * g


def _params(*sem, vmem=VMEM_LIMIT_BYTES):
    return pltpu.CompilerParams(dimension_semantics=sem, vmem_limit_bytes=vmem)


def _whole_block(a):
    zeros = (0,) * a.ndim
    return pl.BlockSpec(a.shape, lambda *_: zeros, pipeline_mode=pl.Buffered(1))


def _layer_block(a, l):
    rest = (0,) * (a.ndim - 1)
    return pl.BlockSpec((None,) + a.shape[1:], lambda *_: (l,) + rest, pipeline_mode=pl.Buffered(1))


def _norm_matmul_kernel(x_ref, g_ref, w_ref, *o_refs):
    h = _rmsnorm(x_ref[...], g_ref[...]).astype(BF16)
    w = w_ref[...]
    if w.dtype != BF16:
        w = w.astype(BF16)
    y = jnp.dot(h, w, preferred_element_type=F32)
    n = y.shape[1] // len(o_refs)
    for i, o_ref in enumerate(o_refs):
        o_ref[...] = y[:, i * n:(i + 1) * n]


def _norm_matmul(x, g, w, l, tm, n_out=1):
    m, k = x.shape
    n = w.shape[2] // n_out
    outs = pl.pallas_call(
        _norm_matmul_kernel,
        grid=(m // tm,),
        in_specs=[pl.BlockSpec((tm, k), lambda i: (i, 0)), _layer_block(g, l), _layer_block(w, l)],
        out_specs=[pl.BlockSpec((tm, n), lambda i: (i, 0))] * n_out,
        out_shape=[jax.ShapeDtypeStruct((m, n), F32)] * n_out,
        compiler_params=_params("arbitrary"),
        name="norm_matmul",
    )(x, g, w)
    return outs[0] if n_out == 1 else outs


def _norm_matmul_cast_kernel(x_ref, g_ref, w_ref, o_ref, wb_ref, h_ref):
    @pl.when(pl.program_id(0) == 0)
    def _():
        h_ref[...] = _rmsnorm(x_ref[...], g_ref[...]).astype(BF16)

    w = w_ref[...].astype(BF16)
    wb_ref[...] = w
    o_ref[...] = jnp.dot(h_ref[...], w, preferred_element_type=F32)


def _norm_matmul_cast(x, g, w, l, tn):
    m, k = x.shape
    n = w.shape[2]
    return pl.pallas_call(
        _norm_matmul_cast_kernel,
        grid=(n // tn,),
        in_specs=[pl.BlockSpec((m, k), lambda j: (0, 0)), _layer_block(g, l),
                  pl.BlockSpec((None, k, tn), lambda j: (l, 0, j))],
        out_specs=[pl.BlockSpec((m, tn), lambda j: (0, j)), pl.BlockSpec((k, tn), lambda j: (0, j))],
        out_shape=[jax.ShapeDtypeStruct((m, n), F32), jax.ShapeDtypeStruct((k, n), BF16)],
        scratch_shapes=[pltpu.VMEM((m, k), BF16)],
        compiler_params=_params("arbitrary"),
        name="norm_matmul_cast",
    )(x, g, w)


def _matmul_norm_res_cast_kernel(a_ref, w_ref, g_ref, x_ref, o_ref, wb_ref):
    kk = pl.program_id(0)
    w = w_ref[...].astype(BF16)
    wb_ref[...] = w
    part = jnp.dot(a_ref[...].astype(BF16), w, preferred_element_type=F32)

    @pl.when(kk == 0)
    def _():
        o_ref[...] = part

    @pl.when(kk > 0)
    def _():
        o_ref[...] += part

    @pl.when(kk == pl.num_programs(0) - 1)
    def _():
        o_ref[...] = x_ref[...] + _rmsnorm(o_ref[...], g_ref[...])


def _matmul_norm_res_cast(a, w, g, x, l, tk):
    m, k = a.shape
    n = w.shape[2]
    return pl.pallas_call(
        _matmul_norm_res_cast_kernel,
        grid=(k // tk,),
        in_specs=[pl.BlockSpec((m, tk), lambda kk: (0, kk)), pl.BlockSpec((None, tk, n), lambda kk: (l, kk, 0)),
                  _layer_block(g, l), pl.BlockSpec((m, n), lambda kk: (0, 0))],
        out_specs=[pl.BlockSpec((m, n), lambda kk: (0, 0)), pl.BlockSpec((tk, n), lambda kk: (kk, 0))],
        out_shape=[jax.ShapeDtypeStruct((m, n), F32), jax.ShapeDtypeStruct((k, n), BF16)],
        compiler_params=_params("arbitrary"),
        name="matmul_norm_res_cast",
    )(a, w, g, x)


def _mlp_kernel(x_ref, gpre_ref, wup_ref, wdown_ref, gpost_ref, o_ref, *rest, cast_weights):
    f = pl.program_id(1)
    h_ref = rest[-1]

    def ff_chunk(first):
        if cast_weights:
            wup_b_ref, wdown_b_ref = rest[:2]
            wup, wdown = wup_ref[...].astype(BF16), wdown_ref[...].astype(BF16)
            wup_b_ref[...] = wup
            wdown_b_ref[...] = wdown
        else:
            wup, wdown = wup_ref[...], wdown_ref[...]
        a = jnp.dot(h_ref[...], wup, preferred_element_type=F32)
        a = jnp.square(jnp.maximum(a, 0.0)).astype(BF16)
        if first:
            o_ref[...] = jnp.dot(a, wdown, preferred_element_type=F32)
        else:
            o_ref[...] += jnp.dot(a, wdown, preferred_element_type=F32)

    @pl.when(f == 0)
    def _():
        h_ref[...] = _rmsnorm(x_ref[...], gpre_ref[...]).astype(BF16)
        ff_chunk(True)

    @pl.when(f > 0)
    def _():
        ff_chunk(False)

    @pl.when(f == pl.num_programs(1) - 1)
    def _():
        o_ref[...] = x_ref[...] + _rmsnorm(o_ref[...], gpost_ref[...])


def _mlp(x, g_pre, w_up, w_down, g_post, l, tm, tf):
    m, d = x.shape
    cast_weights = w_up.ndim == 3
    ff = w_up.shape[-1]
    if cast_weights:
        assert m == tm, "each weight block must be visited exactly once"
        w_specs = [pl.BlockSpec((None, d, tf), lambda i, f: (l, 0, f)),
                   pl.BlockSpec((None, tf, d), lambda i, f: (l, f, 0))]
    else:
        w_specs = [pl.BlockSpec((d, tf), lambda i, f: (0, f)), pl.BlockSpec((tf, d), lambda i, f: (f, 0))]
    out_specs = [pl.BlockSpec((tm, d), lambda i, f: (i, 0))]
    out_shape = [jax.ShapeDtypeStruct((m, d), F32)]
    if cast_weights:
        out_specs += [pl.BlockSpec((d, tf), lambda i, f: (0, f)), pl.BlockSpec((tf, d), lambda i, f: (f, 0))]
        out_shape += [jax.ShapeDtypeStruct((d, ff), BF16), jax.ShapeDtypeStruct((ff, d), BF16)]
    outs = pl.pallas_call(
        functools.partial(_mlp_kernel, cast_weights=cast_weights),
        grid=(m // tm, ff // tf),
        in_specs=[pl.BlockSpec((tm, d), lambda i, f: (i, 0)), _layer_block(g_pre, l)] + w_specs
                 + [_layer_block(g_post, l)],
        out_specs=out_specs,
        out_shape=out_shape,
        scratch_shapes=[pltpu.VMEM((tm, d), BF16)],
        compiler_params=_params("arbitrary", "arbitrary"),
        name="mlp",
    )(x, g_pre, w_up, w_down, g_post)
    return outs if cast_weights else outs[0]


def _pool_delta(uext_ref, u, pos0):
    rows = u.shape[0]
    uext_ref[POOL_HALO:POOL_HALO + rows, :] = u
    return [_pool_group_delta(uext_ref, rows, pos0, gi) for gi in range(len(POOL_WINDOWS))]


def _pool_group_delta(uext_ref, rows, pos0, gi):
    w = POOL_WINDOWS[gi]
    cols = slice(gi * POOL_CH, (gi + 1) * POOL_CH)
    u = uext_ref[POOL_HALO:POOL_HALO + rows, cols]
    s = u
    for back in range(1, w):
        s = s + uext_ref[POOL_HALO - back:POOL_HALO - back + rows, cols]
    pos = pos0 + lax.broadcasted_iota(jnp.int32, (rows, POOL_CH), 0)
    cnt = jnp.minimum(pos + 1, w).astype(F32)
    return s / cnt - u


def _pool_group_project(gi, d, wpool_ref, pscale_ref):
    y = jnp.dot(d.astype(BF16), wpool_ref[gi], preferred_element_type=F32)
    return y * pscale_ref[:, gi * POOL_CH:(gi + 1) * POOL_CH]


def _pool_project(deltas, wpool_ref, pscale_ref):
    return [_pool_group_project(gi, d, wpool_ref, pscale_ref) for gi, d in enumerate(deltas)]


def _conv_mixer(cext_ref, hc, gb, gc, convw_ref):
    rows = hc.shape[0]
    cext_ref[CONV_HALO:CONV_HALO + rows, :] = gc * hc
    conv = convw_ref[0:1, :] * cext_ref[CONV_HALO - 2:CONV_HALO - 2 + rows, :]
    for kk in range(1, CONV_WIDTH):
        lo = CONV_HALO - 2 + kk
        conv = conv + convw_ref[kk:kk + 1, :] * cext_ref[lo:lo + rows, :]
    return gb * conv


def _swa_kv_head(h, q_of_pair, kwin, vwin, sinks_ref, l, first_key):
    nk = kwin.shape[0]
    low = lax.broadcasted_iota(jnp.int32, (nk, LANES), 1) < SWA_HEAD_DIM
    kroll = pltpu.roll(kwin, SWA_HEAD_DIM, 1)
    vroll = pltpu.roll(vwin, SWA_HEAD_DIM, 1)
    ind_a = jnp.where(low, 1.0, 0.0)
    ind_b = 1.0 - ind_a
    if h == 0:
        k_a, k_b = jnp.where(low, kwin, 0.0), jnp.where(low, 0.0, kroll)
        v_a, v_b = jnp.where(low, vwin, 0.0), jnp.where(low, 0.0, vroll)
    else:
        k_a, k_b = jnp.where(low, kroll, 0.0), jnp.where(low, 0.0, kwin)
        v_a, v_b = jnp.where(low, vroll, 0.0), jnp.where(low, 0.0, vwin)
    kst = jnp.concatenate([k_a, k_b], axis=0).astype(BF16)
    vst = jnp.concatenate([jnp.concatenate([v_a, ind_a], axis=1),
                           jnp.concatenate([v_b, ind_b], axis=1)], axis=0).astype(BF16)
    outs = {}
    for r in range(2):
        pair = 2 * h + r
        q = q_of_pair(pair)
        rows = q.shape[0]
        qi = lax.broadcasted_iota(jnp.int32, (rows, 2 * nk), 0)
        kj = lax.broadcasted_iota(jnp.int32, (rows, 2 * nk), 1) & (nk - 1)
        valid = (kj > qi) & (kj <= qi + SWA_WINDOW) & (kj >= first_key)
        s = lax.dot_general((q * SWA_SCALE).astype(BF16), kst, (((1,), (1,)), ((), ())),
                            preferred_element_type=F32)
        s = jnp.where(valid, s, -jnp.inf)
        sink_a = sinks_ref[l, 4 * h + 2 * r]
        sink_b = sinks_ref[l, 4 * h + 2 * r + 1]
        m_a = jnp.maximum(jnp.max(s[:, :nk], axis=1, keepdims=True), sink_a)
        m_b = jnp.maximum(jnp.max(s[:, nk:], axis=1, keepdims=True), sink_b)
        p = jnp.concatenate([jnp.exp(s[:, :nk] - m_a), jnp.exp(s[:, nk:] - m_b)],
                            axis=1).astype(BF16)
        o = jnp.dot(p, vst, preferred_element_type=F32)
        low_q = lax.broadcasted_iota(jnp.int32, (rows, LANES), 1) < SWA_HEAD_DIM
        den = o[:, LANES:] + jnp.where(low_q, jnp.exp(sink_a - m_a), jnp.exp(sink_b - m_b))
        outs[pair] = o[:, :LANES] / den
    return outs


def _mem_head(qm_h, mk_h, mv_h):
    s = lax.dot_general(qm_h.astype(BF16), mk_h, (((1,), (1,)), ((), ())),
                        preferred_element_type=F32) * MEM_SCALE
    p = jnp.exp(s - jnp.max(s, axis=1, keepdims=True)).astype(BF16)
    vext = jnp.concatenate([mv_h, jnp.ones(mv_h.shape, BF16)], axis=1)
    o = jnp.dot(p, vext, preferred_element_type=F32)
    return o[:, :MEM_HEAD_DIM] / o[:, MEM_HEAD_DIM:]


PROJ_CHUNK = 512
N_PROJ_CHUNKS = -(-D_IN // PROJ_CHUNK)


def _prompt_layer_kernel(sinks_ref, xp_ref, xn_ref, gpre_ref, win_ref, mk_ref, mv_ref, wpool_ref, pscale_ref,
                         convw_ref, wout_ref, gpost_ref,
                         o_ref, npool_ref, nconv_ref, nk_ref, nv_ref,
                         proja_ref, projb_ref, h_ref, mixa_ref, mixb_ref, y_ref, uext_ref, cext_ref, kext_ref,
                         vext_ref, *, l, tq, nblk, nsteps):
    r = pl.program_id(0)
    j = r % nblk
    half = tq // 2

    def project_chunk(dst_ref, c):
        cols = slice(c * PROJ_CHUNK, min((c + 1) * PROJ_CHUNK, D_IN))
        dst_ref[:, cols] = jnp.dot(h_ref[...], win_ref[:, cols], preferred_element_type=F32)

    n_out_chunks = wout_ref.shape[1] // PROJ_CHUNK
    n_norm_parts = 4

    def out_chunk(m_ref, c):
        cols = slice(c * PROJ_CHUNK, (c + 1) * PROJ_CHUNK)
        y_ref[:, cols] = jnp.dot(m_ref[...], wout_ref[:, cols], preferred_element_type=F32)

    def finish_half(k):
        rows = slice(k * half, (k + 1) * half)
        o_ref[rows, :] = xp_ref[rows, :] + _rmsnorm(y_ref[rows, :], gpost_ref[...])

    def norm_part(k):
        rows = slice(k * tq // n_norm_parts, (k + 1) * tq // n_norm_parts)
        h_ref[rows, :] = _rmsnorm(xn_ref[rows, :], gpre_ref[...]).astype(BF16)

    @pl.when(r == 0)
    def _():
        h_ref[...] = _rmsnorm(xp_ref[...], gpre_ref[...]).astype(BF16)
        for c in range(N_PROJ_CHUNKS):
            project_chunk(proja_ref, c)
        mixb_ref[...] = jnp.zeros(mixb_ref.shape, BF16)

    @pl.when(j == 0)
    def _():
        uext_ref[0:POOL_HALO, :] = jnp.zeros((POOL_HALO, D_GROUP), F32)
        cext_ref[0:CONV_HALO, :] = jnp.zeros((CONV_HALO, D_GROUP), F32)
        kext_ref[0:SWA_WINDOW, :] = jnp.zeros((SWA_WINDOW, SWA_KV_DIM), F32)
        vext_ref[0:SWA_WINDOW, :] = jnp.zeros((SWA_WINDOW, SWA_KV_DIM), F32)

    def mix_block(p_ref, pn_ref, m_ref, mp_ref):
        proj_chunks = [functools.partial(project_chunk, pn_ref, c) for c in range(N_PROJ_CHUNKS)]

        def put(c0, y, rows=slice(None)):
            m_ref[rows, c0:c0 + y.shape[1]] = y.astype(BF16)

        def pool_fill_piece():
            uext_ref[POOL_HALO:POOL_HALO + tq, :] = p_ref[:, OFF_U:OFF_U + D_GROUP]
            npool_ref[...] = uext_ref[tq + POOL_HALO - POOL_PAD:tq + POOL_HALO, :]

        def pool_group_piece(gi):
            d = _pool_group_delta(uext_ref, tq, j * tq, gi)
            put(MIX_POOL + gi * POOL_CH, _pool_group_project(gi, d, wpool_ref, pscale_ref))

        def pool_carry_piece():
            uext_ref[0:POOL_HALO, :] = uext_ref[tq:tq + POOL_HALO, :]

        def conv_piece():
            y = _conv_mixer(cext_ref, p_ref[:, OFF_HC:OFF_HC + D_GROUP], p_ref[:, OFF_GB:OFF_GB + D_GROUP],
                            p_ref[:, OFF_GC:OFF_GC + D_GROUP], convw_ref)
            put(MIX_CONV, y)
            nconv_ref[...] = cext_ref[tq + CONV_HALO - 2:tq + CONV_HALO, :]
            cext_ref[0:CONV_HALO, :] = cext_ref[tq:tq + CONV_HALO, :]

        def kv_fill_piece():
            kext_ref[SWA_WINDOW:SWA_WINDOW + tq, :] = p_ref[:, OFF_K:OFF_K + SWA_KV_DIM]
            vext_ref[SWA_WINDOW:SWA_WINDOW + tq, :] = p_ref[:, OFF_V:OFF_V + SWA_KV_DIM]
            nk_ref[...] = kext_ref[tq:tq + SWA_WINDOW, :]
            nv_ref[...] = vext_ref[tq:tq + SWA_WINDOW, :]

        def swa_piece(sb, h):
            r0 = sb * SWA_WINDOW
            rows = slice(r0, r0 + SWA_WINDOW)
            first_key = jnp.maximum(SWA_WINDOW - (j * tq + r0), 0)
            q_of_pair = lambda pair: p_ref[rows, OFF_Q + pair * LANES:OFF_Q + (pair + 1) * LANES]
            outs = _swa_kv_head(h, q_of_pair, kext_ref[r0:r0 + 2 * SWA_WINDOW, :],
                                vext_ref[r0:r0 + 2 * SWA_WINDOW, :], sinks_ref, l, first_key)
            for pair, y in outs.items():
                put(MIX_SWA + pair * LANES, y, rows)

        def kv_carry_piece():
            kext_ref[0:SWA_WINDOW, :] = kext_ref[tq:tq + SWA_WINDOW, :]
            vext_ref[0:SWA_WINDOW, :] = vext_ref[tq:tq + SWA_WINDOW, :]

        def mem_piece(hm):
            c0 = hm * MEM_HEAD_DIM
            y = _mem_head(p_ref[:, OFF_QM + c0:OFF_QM + c0 + MEM_HEAD_DIM],
                          mk_ref[:, c0:c0 + MEM_HEAD_DIM].astype(BF16), mv_ref[:, c0:c0 + MEM_HEAD_DIM].astype(BF16))
            put(MIX_MEM + c0, y)

        mxu_items = [functools.partial(out_chunk, mp_ref, c) for c in range(n_out_chunks)] + proj_chunks
        vpu_items = [functools.partial(norm_part, k) for k in range(n_norm_parts)]
        vpu_items += [pool_fill_piece] + [functools.partial(pool_group_piece, gi) for gi in range(len(POOL_WINDOWS))]
        vpu_items += [pool_carry_piece, functools.partial(finish_half, 0), conv_piece,
                      functools.partial(finish_half, 1), kv_fill_piece]
        vpu_items += [functools.partial(swa_piece, sb, h) for sb in range(tq // SWA_WINDOW) for h in range(2)]
        vpu_items += [kv_carry_piece] + [functools.partial(mem_piece, hm) for hm in range(MEM_HEADS)]
        mxu_items[0]()
        rest = mxu_items[1:]
        for i, item in enumerate(vpu_items):
            item()
            for c in range(i * len(rest) // len(vpu_items), (i + 1) * len(rest) // len(vpu_items)):
                rest[c]()

    @pl.when((r < nsteps) & (r % 2 == 0))
    def _():
        mix_block(proja_ref, projb_ref, mixa_ref, mixb_ref)

    @pl.when((r < nsteps) & (r % 2 == 1))
    def _():
        mix_block(projb_ref, proja_ref, mixb_ref, mixa_ref)

    @pl.when(r == nsteps)
    def _():
        for c in range(n_out_chunks):
            out_chunk(mixb_ref if nsteps % 2 == 0 else mixa_ref, c)
        finish_half(0)
        finish_half(1)


def _prompt_layer(sinks, x, g_pre, w_in, mk, mv, w_pool, pool_scale, conv_w, w_out, g_post, l, tq, seq):
    m, d = x.shape
    nblk, nsteps, b = seq // tq, m // tq, m // seq
    nmem = mk.shape[1]
    assert nsteps >= 2
    per_seq = lambda r: (jnp.minimum(r // nblk, b - 1), 0, 0)
    prev_block = lambda r: (jnp.maximum(r - 1, 0), 0)
    return pl.pallas_call(
        functools.partial(_prompt_layer_kernel, l=l, tq=tq, nblk=nblk, nsteps=nsteps),
        grid=(nsteps + 1,),
        in_specs=[pl.BlockSpec(memory_space=pltpu.SMEM),
                  pl.BlockSpec((tq, d), prev_block),
                  pl.BlockSpec((tq, d), lambda r: (jnp.minimum(r + 1, nsteps - 1), 0)),
                  _layer_block(g_pre, l), _whole_block(w_in),
                  pl.BlockSpec((None, nmem, D_GROUP), per_seq),
                  pl.BlockSpec((None, nmem, D_GROUP), per_seq),
                  _layer_block(w_pool, l), _layer_block(pool_scale, l), _layer_block(conv_w, l),
                  _whole_block(w_out), _layer_block(g_post, l)],
        out_specs=[pl.BlockSpec((tq, d), prev_block),
                   pl.BlockSpec((None, POOL_PAD, D_GROUP), per_seq),
                   pl.BlockSpec((None, CONV_WIDTH - 1, D_GROUP), per_seq),
                   pl.BlockSpec((None, SWA_WINDOW, SWA_KV_DIM), per_seq),
                   pl.BlockSpec((None, SWA_WINDOW, SWA_KV_DIM), per_seq)],
        out_shape=[jax.ShapeDtypeStruct((m, d), F32),
                   jax.ShapeDtypeStruct((b, POOL_PAD, D_GROUP), F32),
                   jax.ShapeDtypeStruct((b, CONV_WIDTH - 1, D_GROUP), F32),
                   jax.ShapeDtypeStruct((b, SWA_WINDOW, SWA_KV_DIM), F32),
                   jax.ShapeDtypeStruct((b, SWA_WINDOW, SWA_KV_DIM), F32)],
        scratch_shapes=[pltpu.VMEM((tq, D_IN), F32),
                        pltpu.VMEM((tq, D_IN), F32),
                        pltpu.VMEM((tq, d), BF16),
                        pltpu.VMEM((tq, d), BF16),
                        pltpu.VMEM((tq, d), BF16),
                        pltpu.VMEM((tq, d), F32),
                        pltpu.VMEM((POOL_HALO + tq, D_GROUP), F32),
                        pltpu.VMEM((CONV_HALO + tq, D_GROUP), F32),
                        pltpu.VMEM((SWA_WINDOW + tq, SWA_KV_DIM), F32),
                        pltpu.VMEM((SWA_WINDOW + tq, SWA_KV_DIM), F32)],
        compiler_params=_params("arbitrary", vmem=PROMPT_LAYER_VMEM_BYTES),
        name="prompt_layer",
    )(sinks, x, x, g_pre, w_in, mk, mv, w_pool, pool_scale, conv_w, w_out, g_post)


def _sample_mixer_kernel(sinks_ref, proj_ref, pool_ref, conv_ref, kc_ref, vc_ref, mk_ref, mv_ref,
                         wpool_ref, pscale_ref, convw_ref, *refs, l, nb, t_new, pos0, creates_states):
    mix_ref = refs[-11]
    state_refs = refs[-10:-6]
    if creates_states:
        for ref in state_refs:
            for other in range(ref.shape[0]):
                if other != l:
                    ref[other] = jnp.zeros(ref.shape[1:], F32)
        state_refs = [ref.at[l] for ref in state_refs]
    npool_ref, nconv_ref, nk_ref, nv_ref = state_refs
    uext_all, cext_all, kext_all, vext_all, qs_all, qm_all = refs[-6:]
    uext_all[...] = jnp.zeros(uext_all.shape, F32)
    cext_all[...] = jnp.zeros(cext_all.shape, F32)
    kext_all[...] = jnp.zeros(kext_all.shape, F32)
    vext_all[...] = jnp.zeros(vext_all.shape, F32)

    n_swa = SWA_Q_HEADS * t_new
    n_mem = MEM_HEADS * t_new
    low = lax.broadcasted_iota(jnp.int32, (t_new, LANES), 1) < SWA_HEAD_DIM
    rows_of = lambda a: slice(a * t_new, (a + 1) * t_new)

    deltas = []
    for b in range(nb):
        r = rows_of(b)
        uext_ref, cext_ref, kext_ref, vext_ref = uext_all.at[b], cext_all.at[b], kext_all.at[b], vext_all.at[b]
        uext_ref[POOL_HALO - POOL_PAD:POOL_HALO, :] = pool_ref[b]
        deltas.append(_pool_delta(uext_ref, proj_ref[r, OFF_U:OFF_U + D_GROUP], pos0))
        npool_ref[b] = uext_ref[t_new + POOL_HALO - POOL_PAD:t_new + POOL_HALO, :]

        cext_ref[CONV_HALO - 2:CONV_HALO, :] = conv_ref[b]
        mix_ref[r, MIX_CONV:MIX_CONV + D_GROUP] = _conv_mixer(
            cext_ref, proj_ref[r, OFF_HC:OFF_HC + D_GROUP], proj_ref[r, OFF_GB:OFF_GB + D_GROUP],
            proj_ref[r, OFF_GC:OFF_GC + D_GROUP], convw_ref)
        nconv_ref[b] = cext_ref[t_new + CONV_HALO - 2:t_new + CONV_HALO, :]

        kext_ref[0:SWA_WINDOW, :] = kc_ref[b]
        vext_ref[0:SWA_WINDOW, :] = vc_ref[b]
        kext_ref[SWA_WINDOW:SWA_WINDOW + t_new, :] = proj_ref[r, OFF_K:OFF_K + SWA_KV_DIM]
        vext_ref[SWA_WINDOW:SWA_WINDOW + t_new, :] = proj_ref[r, OFF_V:OFF_V + SWA_KV_DIM]
        nk_ref[b] = kext_ref[t_new:t_new + SWA_WINDOW, :]
        nv_ref[b] = vext_ref[t_new:t_new + SWA_WINDOW, :]
        for pair in range(SWA_Q_HEADS // 2):
            qp = proj_ref[r, OFF_Q + pair * LANES:OFF_Q + (pair + 1) * LANES] * SWA_SCALE
            qr = pltpu.roll(qp, SWA_HEAD_DIM, 1)
            if pair // 2 == 0:
                q_even, q_odd = jnp.where(low, qp, 0.0), jnp.where(low, qr, 0.0)
            else:
                q_even, q_odd = jnp.where(low, 0.0, qr), jnp.where(low, 0.0, qp)
            qs_all[b, rows_of(2 * pair), :] = q_even
            qs_all[b, rows_of(2 * pair + 1), :] = q_odd
        for hm in range(MEM_HEADS):
            qm_all[b, rows_of(hm), :] = (
                proj_ref[r, OFF_QM + hm * MEM_HEAD_DIM:OFF_QM + (hm + 1) * MEM_HEAD_DIM])

    deltas = [jnp.concatenate([d[gi] for d in deltas], axis=0) for gi in range(len(POOL_WINDOWS))]
    for gi, y in enumerate(_pool_project(deltas, wpool_ref, pscale_ref)):
        mix_ref[:, MIX_POOL + gi * POOL_CH:MIX_POOL + (gi + 1) * POOL_CH] = y

    nt_dims = (((1,), (1,)), ((), ()))
    s_swa = jnp.concatenate(
        [lax.dot_general(qs_all[b].astype(BF16), kext_all[b].astype(BF16),
                         nt_dims, preferred_element_type=F32) for b in range(nb)], axis=0)
    s_mem = jnp.concatenate(
        [lax.dot_general(qm_all[b].astype(BF16), mk_ref[b].astype(BF16),
                         nt_dims, preferred_element_type=F32) for b in range(nb)], axis=0)

    ri = lax.broadcasted_iota(jnp.int32, s_swa.shape, 0)
    tok = ri % t_new
    kj = lax.broadcasted_iota(jnp.int32, s_swa.shape, 1)
    valid = (kj > tok) & (kj <= tok + SWA_WINDOW) & (kj >= max(SWA_WINDOW - pos0, 0))
    head = (lax.broadcasted_iota(jnp.int32, (s_swa.shape[0], 1), 0) // t_new) % SWA_Q_HEADS
    sink = jnp.zeros(head.shape, F32)
    for hq in range(SWA_Q_HEADS):
        sink = jnp.where(head == hq, sinks_ref[l, hq], sink)
    s_swa = jnp.where(valid, s_swa, -jnp.inf)
    m = jnp.maximum(jnp.max(s_swa, axis=1, keepdims=True), sink)
    e = jnp.exp(s_swa - m)
    p_swa = (e * (1.0 / (jnp.sum(e, axis=1, keepdims=True) + jnp.exp(sink - m)))).astype(BF16)

    mem_valid = (lax.broadcasted_iota(jnp.int32, s_mem.shape, 1) % MEM_HEADS
                 == (lax.broadcasted_iota(jnp.int32, s_mem.shape, 0) // t_new) % MEM_HEADS)
    s_mem = jnp.where(mem_valid, s_mem * MEM_SCALE, -jnp.inf)
    e = jnp.exp(s_mem - jnp.max(s_mem, axis=1, keepdims=True))
    p_mem = (e * (1.0 / jnp.sum(e, axis=1, keepdims=True))).astype(BF16)

    for b in range(nb):
        r = rows_of(b)
        o = jnp.dot(p_swa[b * n_swa:(b + 1) * n_swa, :], vext_all[b].astype(BF16), preferred_element_type=F32)
        o_roll = pltpu.roll(o, SWA_HEAD_DIM, 1)
        for pair in range(SWA_Q_HEADS // 2):
            src_even, src_odd = (o, o_roll) if pair // 2 == 0 else (o_roll, o)
            mix_ref[r, MIX_SWA + pair * LANES:MIX_SWA + (pair + 1) * LANES] = jnp.where(
                low, src_even[rows_of(2 * pair), :], src_odd[rows_of(2 * pair + 1), :])
        o = jnp.dot(p_mem[b * n_mem:(b + 1) * n_mem, :], mv_ref[b].astype(BF16), preferred_element_type=F32)
        for hm in range(MEM_HEADS):
            mix_ref[r, MIX_MEM + hm * MEM_HEAD_DIM:MIX_MEM + (hm + 1) * MEM_HEAD_DIM] = o[rows_of(hm), :]


def _sample_mixer(sinks, proj, pool, conv, kc, vc, mk, mv, w_pool, pool_scale, conv_w, new_states,
                  l, nb, t_new, pos0):
    b = proj.shape[0] // t_new
    slots = nb
    blk = lambda a: pl.BlockSpec((None, nb) + a.shape[2:], lambda i: (l, i) + (0,) * (a.ndim - 2))
    out_sds = lambda a: jax.ShapeDtypeStruct(a.shape, F32)
    n_in = 11
    carried = [] if new_states is None else list(new_states)
    if carried:
        state_blk = blk
    else:
        state_blk = lambda a: pl.BlockSpec((a.shape[0], nb) + a.shape[2:], lambda i: (0, i) + (0,) * (a.ndim - 2))
    return pl.pallas_call(
        functools.partial(_sample_mixer_kernel, l=l, nb=nb, t_new=t_new, pos0=pos0, creates_states=not carried),
        grid=(b // nb,),
        in_specs=[pl.BlockSpec(memory_space=pltpu.SMEM),
                  pl.BlockSpec((nb * t_new, D_IN), lambda i: (i, 0)),
                  blk(pool), blk(conv), blk(kc), blk(vc), blk(mk), blk(mv),
                  _layer_block(w_pool, l), _layer_block(pool_scale, l), _layer_block(conv_w, l)]
                 + [pl.BlockSpec(memory_space=pl.ANY)] * len(carried),
        input_output_aliases={n_in + k: 1 + k for k in range(len(carried))},
        out_specs=[pl.BlockSpec((nb * t_new, 4 * D_GROUP), lambda i: (i, 0)),
                   state_blk(pool), state_blk(conv), state_blk(kc), state_blk(vc)],
        out_shape=[jax.ShapeDtypeStruct((b * t_new, 4 * D_GROUP), F32),
                   out_sds(pool), out_sds(conv), out_sds(kc), out_sds(vc)],
        scratch_shapes=[pltpu.VMEM((slots, POOL_HALO + 8, D_GROUP), F32),
                        pltpu.VMEM((slots, CONV_HALO + 8, D_GROUP), F32),
                        pltpu.VMEM((slots, 2 * SWA_WINDOW, SWA_KV_DIM), F32),
                        pltpu.VMEM((slots, 2 * SWA_WINDOW, SWA_KV_DIM), F32),
                        pltpu.VMEM((slots, SWA_Q_HEADS * t_new, LANES), F32),
                        pltpu.VMEM((slots, MEM_HEADS * t_new, MEM_HEAD_DIM), F32)],
        compiler_params=_params("arbitrary"),
        name="sample_mixer",
    )(sinks, proj, pool, conv, kc, vc, mk, mv, w_pool, pool_scale, conv_w, *carried)


def kernel(x_prompt, x_sample, mem_prompt, state_pool, state_conv, cache_swa_k, cache_swa_v,
           cache_mem_k, cache_mem_v, g_mix_pre, w_in, w_pool, pool_scale, conv_w, swa_sinks,
           g_mem, w_mem_kv, w_out, g_mix_post, g_mlp_pre, w_up, w_down, g_mlp_post):
    depth = w_in.shape[0]
    bp, seq, d_model = x_prompt.shape
    bs, t_new, _ = x_sample.shape
    nmem = mem_prompt.shape[1]
    assert w_in.shape[2] == D_IN and d_model == 4 * D_GROUP

    w_pool_b = w_pool.astype(BF16)
    rows = lambda a: a.reshape(depth, 1, a.shape[-1])
    g_mix_pre, g_mem, g_mix_post = rows(g_mix_pre), rows(g_mem), rows(g_mix_post)
    g_mlp_pre, g_mlp_post, pool_scale = rows(g_mlp_pre), rows(g_mlp_post), rows(pool_scale)

    yp = x_prompt.reshape(bp * seq, d_model)
    ys = x_sample.reshape(bs * t_new, d_model)
    mem = mem_prompt.reshape(bp * nmem, d_model)
    kc = cache_swa_k.reshape(depth, bs, SWA_WINDOW, SWA_KV_DIM)
    vc = cache_swa_v.reshape(depth, bs, SWA_WINDOW, SWA_KV_DIM)
    mkc = cache_mem_k.reshape(depth, bs, nmem * MEM_HEADS, MEM_HEAD_DIM)
    mvc = cache_mem_v.reshape(depth, bs, nmem * MEM_HEADS, MEM_HEAD_DIM)

    outs = [[] for _ in range(6)]
    sample_states = None
    for l in range(depth):
        proj, w_in_b = _norm_matmul_cast(ys, g_mix_pre, w_in, l, tn=256)
        mix, *sample_states = _sample_mixer(
            swa_sinks, proj, state_pool, state_conv, kc, vc, mkc, mvc, w_pool_b, pool_scale, conv_w,
            sample_states, l, nb=8, t_new=t_new, pos0=PAST_LEN)
        ys, w_out_b = _matmul_norm_res_cast(mix, w_out, g_mix_post, ys, l, tk=512)
        ys, w_up_b, w_down_b = _mlp(ys, g_mlp_pre, w_up, w_down, g_mlp_post, l, tm=bs * t_new, tf=512)

        mk, mv = _norm_matmul(mem, g_mem, w_mem_kv, l, tm=1024, n_out=2)
        mk, mv = mk.reshape(bp, nmem, D_GROUP), mv.reshape(bp, nmem, D_GROUP)
        yp, pool_p, conv_p, k_p, v_p = _prompt_layer(swa_sinks, yp, g_mix_pre, w_in_b, mk, mv, w_pool_b, pool_scale,
                                                     conv_w, w_out_b, g_mix_post, l, tq=256, seq=seq)
        yp = _mlp(yp, g_mlp_pre, w_up_b, w_down_b, g_mlp_post, l, tm=1024, tf=512)

        for lst, val in zip(outs, (pool_p, conv_p, k_p, v_p, mk, mv)):
            lst.append(val)

    pool_p, conv_p, k_p, v_p, mk, mv = (jnp.stack(o) for o in outs)
    pool_s, conv_s, k_s, v_s = sample_states
    kv_shape = lambda a: a.reshape(a.shape[:2] + (SWA_WINDOW, 2, SWA_HEAD_DIM))
    mem_shape = lambda a: a.reshape(depth, bp, nmem, MEM_HEADS, MEM_HEAD_DIM)
    return (yp.reshape(bp, seq, d_model), ys.reshape(bs, t_new, d_model), pool_p, pool_s, conv_p, conv_s,
            kv_shape(k_p), kv_shape(k_s), kv_shape(v_p), kv_shape(v_s), mem_shape(mk), mem_shape(mv))
```

```python
import functools
import math

import jax
import jax.numpy as jnp
from jax import lax
from jax.experimental import pallas as pl
from jax.experimental.pallas import tpu as pltpu

F32 = jnp.float32
BF16 = jnp.bfloat16

LANES = 128
D_GROUP = 512
POOL_WINDOWS = (2, 4, 8, 16)
POOL_CH = D_GROUP // len(POOL_WINDOWS)
POOL_PAD = max(POOL_WINDOWS) - 1
POOL_HALO = 16
CONV_WIDTH = 3
CONV_HALO = 8
SWA_WINDOW = 128
SWA_HEAD_DIM = 64
SWA_Q_HEADS = 8
SWA_GROUP = 4
SWA_KV_DIM = 128
SWA_SCALE = 1.0 / math.sqrt(SWA_HEAD_DIM)
MEM_HEADS = 4
MEM_HEAD_DIM = D_GROUP // MEM_HEADS
MEM_SCALE = 1.0 / math.sqrt(MEM_HEAD_DIM)
RMS_EPS = 1e-6
PAST_LEN = 8192
OFF_U, OFF_HC, OFF_GB, OFF_GC, OFF_Q = 0, 512, 1024, 1536, 2048
OFF_K, OFF_V, OFF_QM, D_IN = 2560, 2688, 2816, 3328
MIX_POOL, MIX_CONV, MIX_SWA, MIX_MEM = 0, 512, 1024, 1536

VMEM_LIMIT_BYTES = 56 * 1024 * 1024
PROMPT_LAYER_VMEM_BYTES = 60 * 1024 * 1024


def _rmsnorm(x, g):
    ms = jnp.mean(x * x, axis=-1, keepdims=True)
    return x * lax.rsqrt(ms + RMS_EPS) * g


def _params(*sem, vmem=VMEM_LIMIT_BYTES):
    return pltpu.CompilerParams(dimension_semantics=sem, vmem_limit_bytes=vmem)


def _whole_block(a):
    zeros = (0,) * a.ndim
    return pl.BlockSpec(a.shape, lambda *_: zeros, pipeline_mode=pl.Buffered(1))


def _layer_block(a, l):
    rest = (0,) * (a.ndim - 1)
    return pl.BlockSpec((None,) + a.shape[1:], lambda *_: (l,) + rest, pipeline_mode=pl.Buffered(1))


def _norm_matmul_kernel(x_ref, g_ref, w_ref, *o_refs):
    h = _rmsnorm(x_ref[...], g_ref[...]).astype(BF16)
    w = w_ref[...]
    if w.dtype != BF16:
        w = w.astype(BF16)
    y = jnp.dot(h, w, preferred_element_type=F32)
    n = y.shape[1] // len(o_refs)
    for i, o_ref in enumerate(o_refs):
        o_ref[...] = y[:, i * n:(i + 1) * n]


def _norm_matmul(x, g, w, l, tm, n_out=1):
    m, k = x.shape
    n = w.shape[2] // n_out
    outs = pl.pallas_call(
        _norm_matmul_kernel,
        grid=(m // tm,),
        in_specs=[pl.BlockSpec((tm, k), lambda i: (i, 0)), _layer_block(g, l), _layer_block(w, l)],
        out_specs=[pl.BlockSpec((tm, n), lambda i: (i, 0))] * n_out,
        out_shape=[jax.ShapeDtypeStruct((m, n), F32)] * n_out,
        compiler_params=_params("arbitrary"),
        name="norm_matmul",
    )(x, g, w)
    return outs[0] if n_out == 1 else outs


def _norm_matmul_cast_kernel(x_ref, g_ref, w_ref, o_ref, wb_ref, h_ref):
    @pl.when(pl.program_id(0) == 0)
    def _():
        h_ref[...] = _rmsnorm(x_ref[...], g_ref[...]).astype(BF16)

    w = w_ref[...].astype(BF16)
    wb_ref[...] = w
    o_ref[...] = jnp.dot(h_ref[...], w, preferred_element_type=F32)


def _norm_matmul_cast(x, g, w, l, tn):
    m, k = x.shape
    n = w.shape[2]
    return pl.pallas_call(
        _norm_matmul_cast_kernel,
        grid=(n // tn,),
        in_specs=[pl.BlockSpec((m, k), lambda j: (0, 0)), _layer_block(g, l),
                  pl.BlockSpec((None, k, tn), lambda j: (l, 0, j))],
        out_specs=[pl.BlockSpec((m, tn), lambda j: (0, j)), pl.BlockSpec((k, tn), lambda j: (0, j))],
        out_shape=[jax.ShapeDtypeStruct((m, n), F32), jax.ShapeDtypeStruct((k, n), BF16)],
        scratch_shapes=[pltpu.VMEM((m, k), BF16)],
        compiler_params=_params("arbitrary"),
        name="norm_matmul_cast",
    )(x, g, w)


def _matmul_norm_res_cast_kernel(a_ref, w_ref, g_ref, x_ref, o_ref, wb_ref):
    kk = pl.program_id(0)
    w = w_ref[...].astype(BF16)
    wb_ref[...] = w
    part = jnp.dot(a_ref[...].astype(BF16), w, preferred_element_type=F32)

    @pl.when(kk == 0)
    def _():
        o_ref[...] = part

    @pl.when(kk > 0)
    def _():
        o_ref[...] += part

    @pl.when(kk == pl.num_programs(0) - 1)
    def _():
        o_ref[...] = x_ref[...] + _rmsnorm(o_ref[...], g_ref[...])


def _matmul_norm_res_cast(a, w, g, x, l, tk):
    m, k = a.shape
    n = w.shape[2]
    return pl.pallas_call(
        _matmul_norm_res_cast_kernel,
        grid=(k // tk,),
        in_specs=[pl.BlockSpec((m, tk), lambda kk: (0, kk)), pl.BlockSpec((None, tk, n), lambda kk: (l, kk, 0)),
                  _layer_block(g, l), pl.BlockSpec((m, n), lambda kk: (0, 0))],
        out_specs=[pl.BlockSpec((m, n), lambda kk: (0, 0)), pl.BlockSpec((tk, n), lambda kk: (kk, 0))],
        out_shape=[jax.ShapeDtypeStruct((m, n), F32), jax.ShapeDtypeStruct((k, n), BF16)],
        compiler_params=_params("arbitrary"),
        name="matmul_norm_res_cast",
    )(a, w, g, x)


def _mlp_kernel(x_ref, gpre_ref, wup_ref, wdown_ref, gpost_ref, o_ref, *rest, cast_weights):
    f = pl.program_id(1)
    h_ref = rest[-1]

    def ff_chunk(first):
        if cast_weights:
            wup_b_ref, wdown_b_ref = rest[:2]
            wup, wdown = wup_ref[...].astype(BF16), wdown_ref[...].astype(BF16)
            wup_b_ref[...] = wup
            wdown_b_ref[...] = wdown
        else:
            wup, wdown = wup_ref[...], wdown_ref[...]
        a = jnp.dot(h_ref[...], wup, preferred_element_type=F32)
        a = jnp.square(jnp.maximum(a, 0.0)).astype(BF16)
        if first:
            o_ref[...] = jnp.dot(a, wdown, preferred_element_type=F32)
        else:
            o_ref[...] += jnp.dot(a, wdown, preferred_element_type=F32)

    @pl.when(f == 0)
    def _():
        h_ref[...] = _rmsnorm(x_ref[...], gpre_ref[...]).astype(BF16)
        ff_chunk(True)

    last = pl.num_programs(1) - 1

    @pl.when((f > 0) & (f < last))
    def _():
        ff_chunk(False)

    @pl.when(f == last)
    def _():
        ff_chunk(False)
        o_ref[...] = x_ref[...] + _rmsnorm(o_ref[...], gpost_ref[...])


def _mlp(x, g_pre, w_up, w_down, g_post, l, tm, tf):
    m, d = x.shape
    cast_weights = w_up.ndim == 3
    ff = w_up.shape[-1]
    if cast_weights:
        assert m == tm, "each weight block must be visited exactly once"
        w_specs = [pl.BlockSpec((None, d, tf), lambda i, f: (l, 0, f)),
                   pl.BlockSpec((None, tf, d), lambda i, f: (l, f, 0))]
    else:
        w_specs = [pl.BlockSpec((d, tf), lambda i, f: (0, f)), pl.BlockSpec((tf, d), lambda i, f: (f, 0))]
    out_specs = [pl.BlockSpec((tm, d), lambda i, f: (i, 0))]
    out_shape = [jax.ShapeDtypeStruct((m, d), F32)]
    if cast_weights:
        out_specs += [pl.BlockSpec((d, tf), lambda i, f: (0, f)), pl.BlockSpec((tf, d), lambda i, f: (f, 0))]
        out_shape += [jax.ShapeDtypeStruct((d, ff), BF16), jax.ShapeDtypeStruct((ff, d), BF16)]
    outs = pl.pallas_call(
        functools.partial(_mlp_kernel, cast_weights=cast_weights),
        grid=(m // tm, ff // tf),
        in_specs=[pl.BlockSpec((tm, d), lambda i, f: (i, 0)), _layer_block(g_pre, l)] + w_specs
                 + [_layer_block(g_post, l)],
        out_specs=out_specs,
        out_shape=out_shape,
        scratch_shapes=[pltpu.VMEM((tm, d), BF16)],
        compiler_params=_params("arbitrary", "arbitrary"),
        name="mlp",
    )(x, g_pre, w_up, w_down, g_post)
    return outs if cast_weights else outs[0]


def _pool_delta(uext_ref, u, pos0):
    rows = u.shape[0]
    uext_ref[POOL_HALO:POOL_HALO + rows, :] = u
    return [_pool_group_delta(uext_ref, rows, pos0, gi) for gi in range(len(POOL_WINDOWS))]


def _pool_group_delta(uext_ref, rows, pos0, gi):
    w = POOL_WINDOWS[gi]
    cols = slice(gi * POOL_CH, (gi + 1) * POOL_CH)
    u = uext_ref[POOL_HALO:POOL_HALO + rows, cols]
    s = u
    for back in range(1, w):
        s = s + uext_ref[POOL_HALO - back:POOL_HALO - back + rows, cols]
    pos = pos0 + lax.broadcasted_iota(jnp.int32, (rows, POOL_CH), 0)
    cnt = jnp.minimum(pos + 1, w).astype(F32)
    return s / cnt - u


def _pool_group_project(gi, d, wpool_ref, pscale_ref):
    y = jnp.dot(d.astype(BF16), wpool_ref[gi], preferred_element_type=F32)
    return y * pscale_ref[:, gi * POOL_CH:(gi + 1) * POOL_CH]


def _pool_project(deltas, wpool_ref, pscale_ref):
    return [_pool_group_project(gi, d, wpool_ref, pscale_ref) for gi, d in enumerate(deltas)]


def _conv_mixer(cext_ref, hc, gb, gc, convw_ref):
    rows = hc.shape[0]
    cext_ref[CONV_HALO:CONV_HALO + rows, :] = gc * hc
    conv = convw_ref[0:1, :] * cext_ref[CONV_HALO - 2:CONV_HALO - 2 + rows, :]
    for kk in range(1, CONV_WIDTH):
        lo = CONV_HALO - 2 + kk
        conv = conv + convw_ref[kk:kk + 1, :] * cext_ref[lo:lo + rows, :]
    return gb * conv


def _swa_kv_head(h, q_of_pair, kwin, vwin, sinks_ref, l, first_key):
    nk = kwin.shape[0]
    low = lax.broadcasted_iota(jnp.int32, (nk, LANES), 1) < SWA_HEAD_DIM
    kroll = pltpu.roll(kwin, SWA_HEAD_DIM, 1)
    vroll = pltpu.roll(vwin, SWA_HEAD_DIM, 1)
    ind_a = jnp.where(low, 1.0, 0.0)
    ind_b = 1.0 - ind_a
    if h == 0:
        k_a, k_b = jnp.where(low, kwin, 0.0), jnp.where(low, 0.0, kroll)
        v_a, v_b = jnp.where(low, vwin, 0.0), jnp.where(low, 0.0, vroll)
    else:
        k_a, k_b = jnp.where(low, kroll, 0.0), jnp.where(low, 0.0, kwin)
        v_a, v_b = jnp.where(low, vroll, 0.0), jnp.where(low, 0.0, vwin)
    kst = jnp.concatenate([k_a, k_b], axis=0).astype(BF16)
    vst = jnp.concatenate([jnp.concatenate([v_a, ind_a], axis=1),
                           jnp.concatenate([v_b, ind_b], axis=1)], axis=0).astype(BF16)
    outs = {}
    for r in range(2):
        pair = 2 * h + r
        q = q_of_pair(pair)
        rows = q.shape[0]
        qi = lax.broadcasted_iota(jnp.int32, (rows, 2 * nk), 0)
        kj = lax.broadcasted_iota(jnp.int32, (rows, 2 * nk), 1) & (nk - 1)
        valid = (kj > qi) & (kj <= qi + SWA_WINDOW) & (kj >= first_key)
        s = lax.dot_general((q * SWA_SCALE).astype(BF16), kst, (((1,), (1,)), ((), ())),
                            preferred_element_type=F32)
        s = jnp.where(valid, s, -jnp.inf)
        sink_a = sinks_ref[l, 4 * h + 2 * r]
        sink_b = sinks_ref[l, 4 * h + 2 * r + 1]
        m_a = jnp.maximum(jnp.max(s[:, :nk], axis=1, keepdims=True), sink_a)
        m_b = jnp.maximum(jnp.max(s[:, nk:], axis=1, keepdims=True), sink_b)
        p = jnp.concatenate([jnp.exp(s[:, :nk] - m_a), jnp.exp(s[:, nk:] - m_b)],
                            axis=1).astype(BF16)
        o = jnp.dot(p, vst, preferred_element_type=F32)
        low_q = lax.broadcasted_iota(jnp.int32, (rows, LANES), 1) < SWA_HEAD_DIM
        den = o[:, LANES:] + jnp.where(low_q, jnp.exp(sink_a - m_a), jnp.exp(sink_b - m_b))
        outs[pair] = o[:, :LANES] / den
    return outs


def _mem_head(qm_h, mk_h, mv_h):
    s = lax.dot_general(qm_h.astype(BF16), mk_h, (((1,), (1,)), ((), ())),
                        preferred_element_type=F32) * MEM_SCALE
    p = jnp.exp(s - jnp.max(s, axis=1, keepdims=True)).astype(BF16)
    vext = jnp.concatenate([mv_h, jnp.ones(mv_h.shape, BF16)], axis=1)
    o = jnp.dot(p, vext, preferred_element_type=F32)
    return o[:, :MEM_HEAD_DIM] / o[:, MEM_HEAD_DIM:]


PROJ_CHUNK = 512
N_PROJ_CHUNKS = -(-D_IN // PROJ_CHUNK)


def _prompt_layer_kernel(sinks_ref, xp_ref, xn_ref, gpre_ref, win_ref, mk_ref, mv_ref, wpool_ref, pscale_ref,
                         convw_ref, wout_ref, gpost_ref,
                         o_ref, npool_ref, nconv_ref, nk_ref, nv_ref,
                         proja_ref, projb_ref, h_ref, mixa_ref, mixb_ref, y_ref, uext_ref, cext_ref, kext_ref,
                         vext_ref, *, l, tq, nblk, nsteps):
    r = pl.program_id(0)
    j = r % nblk
    half = tq // 2

    def project_chunk(dst_ref, c):
        cols = slice(c * PROJ_CHUNK, min((c + 1) * PROJ_CHUNK, D_IN))
        dst_ref[:, cols] = jnp.dot(h_ref[...], win_ref[:, cols], preferred_element_type=F32)

    n_out_chunks = wout_ref.shape[1] // PROJ_CHUNK
    n_norm_parts = 4

    def out_chunk(m_ref, c):
        cols = slice(c * PROJ_CHUNK, (c + 1) * PROJ_CHUNK)
        y_ref[:, cols] = jnp.dot(m_ref[...], wout_ref[:, cols], preferred_element_type=F32)

    def finish_half(k):
        rows = slice(k * half, (k + 1) * half)
        o_ref[rows, :] = xp_ref[rows, :] + _rmsnorm(y_ref[rows, :], gpost_ref[...])

    def norm_part(k):
        rows = slice(k * tq // n_norm_parts, (k + 1) * tq // n_norm_parts)
        h_ref[rows, :] = _rmsnorm(xn_ref[rows, :], gpre_ref[...]).astype(BF16)

    @pl.when(r == 0)
    def _():
        h_ref[...] = _rmsnorm(xp_ref[...], gpre_ref[...]).astype(BF16)
        for c in range(N_PROJ_CHUNKS):
            project_chunk(proja_ref, c)
        mixb_ref[...] = jnp.zeros(mixb_ref.shape, BF16)

    @pl.when(j == 0)
    def _():
        uext_ref[0:POOL_HALO, :] = jnp.zeros((POOL_HALO, D_GROUP), F32)
        cext_ref[0:CONV_HALO, :] = jnp.zeros((CONV_HALO, D_GROUP), F32)
        kext_ref[0:SWA_WINDOW, :] = jnp.zeros((SWA_WINDOW, SWA_KV_DIM), F32)
        vext_ref[0:SWA_WINDOW, :] = jnp.zeros((SWA_WINDOW, SWA_KV_DIM), F32)

    def mix_block(p_ref, pn_ref, m_ref, mp_ref):
        proj_chunks = [functools.partial(project_chunk, pn_ref, c) for c in range(N_PROJ_CHUNKS)]

        def put(c0, y, rows=slice(None)):
            m_ref[rows, c0:c0 + y.shape[1]] = y.astype(BF16)

        def pool_fill_piece():
            uext_ref[POOL_HALO:POOL_HALO + tq, :] = p_ref[:, OFF_U:OFF_U + D_GROUP]
            npool_ref[...] = uext_ref[tq + POOL_HALO - POOL_PAD:tq + POOL_HALO, :]

        def pool_group_piece(gi):
            d = _pool_group_delta(uext_ref, tq, j * tq, gi)
            put(MIX_POOL + gi * POOL_CH, _pool_group_project(gi, d, wpool_ref, pscale_ref))

        def pool_carry_piece():
            uext_ref[0:POOL_HALO, :] = uext_ref[tq:tq + POOL_HALO, :]

        def conv_piece():
            y = _conv_mixer(cext_ref, p_ref[:, OFF_HC:OFF_HC + D_GROUP], p_ref[:, OFF_GB:OFF_GB + D_GROUP],
                            p_ref[:, OFF_GC:OFF_GC + D_GROUP], convw_ref)
            put(MIX_CONV, y)
            nconv_ref[...] = cext_ref[tq + CONV_HALO - 2:tq + CONV_HALO, :]
            cext_ref[0:CONV_HALO, :] = cext_ref[tq:tq + CONV_HALO, :]

        def kv_fill_piece():
            kext_ref[SWA_WINDOW:SWA_WINDOW + tq, :] = p_ref[:, OFF_K:OFF_K + SWA_KV_DIM]
            vext_ref[SWA_WINDOW:SWA_WINDOW + tq, :] = p_ref[:, OFF_V:OFF_V + SWA_KV_DIM]
            nk_ref[...] = kext_ref[tq:tq + SWA_WINDOW, :]
            nv_ref[...] = vext_ref[tq:tq + SWA_WINDOW, :]

        def swa_piece(sb, h):
            r0 = sb * SWA_WINDOW
            rows = slice(r0, r0 + SWA_WINDOW)
            first_key = jnp.maximum(SWA_WINDOW - (j * tq + r0), 0)
            q_of_pair = lambda pair: p_ref[rows, OFF_Q + pair * LANES:OFF_Q + (pair + 1) * LANES]
            outs = _swa_kv_head(h, q_of_pair, kext_ref[r0:r0 + 2 * SWA_WINDOW, :],
                                vext_ref[r0:r0 + 2 * SWA_WINDOW, :], sinks_ref, l, first_key)
            for pair, y in outs.items():
                put(MIX_SWA + pair * LANES, y, rows)

        def kv_carry_piece():
            kext_ref[0:SWA_WINDOW, :] = kext_ref[tq:tq + SWA_WINDOW, :]
            vext_ref[0:SWA_WINDOW, :] = vext_ref[tq:tq + SWA_WINDOW, :]

        def mem_piece(hm):
            c0 = hm * MEM_HEAD_DIM
            y = _mem_head(p_ref[:, OFF_QM + c0:OFF_QM + c0 + MEM_HEAD_DIM],
                          mk_ref[:, c0:c0 + MEM_HEAD_DIM].astype(BF16), mv_ref[:, c0:c0 + MEM_HEAD_DIM].astype(BF16))
            put(MIX_MEM + c0, y)

        mxu_items = [functools.partial(out_chunk, mp_ref, c) for c in range(n_out_chunks)] + proj_chunks
        vpu_items = [functools.partial(norm_part, k) for k in range(n_norm_parts)]
        vpu_items += [pool_fill_piece] + [functools.partial(pool_group_piece, gi) for gi in range(len(POOL_WINDOWS))]
        vpu_items += [pool_carry_piece, functools.partial(finish_half, 0), conv_piece,
                      functools.partial(finish_half, 1), kv_fill_piece]
        vpu_items += [functools.partial(swa_piece, sb, h) for sb in range(tq // SWA_WINDOW) for h in range(2)]
        vpu_items += [kv_carry_piece] + [functools.partial(mem_piece, hm) for hm in range(MEM_HEADS)]
        mxu_items[0]()
        rest = mxu_items[1:]
        for i, item in enumerate(vpu_items):
            item()
            for c in range(i * len(rest) // len(vpu_items), (i + 1) * len(rest) // len(vpu_items)):
                rest[c]()

    @pl.when((r < nsteps) & (r % 2 == 0))
    def _():
        mix_block(proja_ref, projb_ref, mixa_ref, mixb_ref)

    @pl.when((r < nsteps) & (r % 2 == 1))
    def _():
        mix_block(projb_ref, proja_ref, mixb_ref, mixa_ref)

    @pl.when(r == nsteps)
    def _():
        for c in range(n_out_chunks):
            out_chunk(mixb_ref if nsteps % 2 == 0 else mixa_ref, c)
        finish_half(0)
        finish_half(1)


def _prompt_layer(sinks, x, g_pre, w_in, mk, mv, w_pool, pool_scale, conv_w, w_out, g_post, l, tq, seq):
    m, d = x.shape
    nblk, nsteps, b = seq // tq, m // tq, m // seq
    nmem = mk.shape[1]
    assert nsteps >= 2
    per_seq = lambda r: (jnp.minimum(r // nblk, b - 1), 0, 0)
    prev_block = lambda r: (jnp.maximum(r - 1, 0), 0)
    return pl.pallas_call(
        functools.partial(_prompt_layer_kernel, l=l, tq=tq, nblk=nblk, nsteps=nsteps),
        grid=(nsteps + 1,),
        in_specs=[pl.BlockSpec(memory_space=pltpu.SMEM),
                  pl.BlockSpec((tq, d), prev_block),
                  pl.BlockSpec((tq, d), lambda r: (jnp.minimum(r + 1, nsteps - 1), 0)),
                  _layer_block(g_pre, l), _whole_block(w_in),
                  pl.BlockSpec((None, nmem, D_GROUP), per_seq),
                  pl.BlockSpec((None, nmem, D_GROUP), per_seq),
                  _layer_block(w_pool, l), _layer_block(pool_scale, l), _layer_block(conv_w, l),
                  _whole_block(w_out), _layer_block(g_post, l)],
        out_specs=[pl.BlockSpec((tq, d), prev_block),
                   pl.BlockSpec((None, POOL_PAD, D_GROUP), per_seq),
                   pl.BlockSpec((None, CONV_WIDTH - 1, D_GROUP), per_seq),
                   pl.BlockSpec((None, SWA_WINDOW, SWA_KV_DIM), per_seq),
                   pl.BlockSpec((None, SWA_WINDOW, SWA_KV_DIM), per_seq)],
        out_shape=[jax.ShapeDtypeStruct((m, d), F32),
                   jax.ShapeDtypeStruct((b, POOL_PAD, D_GROUP), F32),
                   jax.ShapeDtypeStruct((b, CONV_WIDTH - 1, D_GROUP), F32),
                   jax.ShapeDtypeStruct((b, SWA_WINDOW, SWA_KV_DIM), F32),
                   jax.ShapeDtypeStruct((b, SWA_WINDOW, SWA_KV_DIM), F32)],
        scratch_shapes=[pltpu.VMEM((tq, D_IN), F32),
                        pltpu.VMEM((tq, D_IN), F32),
                        pltpu.VMEM((tq, d), BF16),
                        pltpu.VMEM((tq, d), BF16),
                        pltpu.VMEM((tq, d), BF16),
                        pltpu.VMEM((tq, d), F32),
                        pltpu.VMEM((POOL_HALO + tq, D_GROUP), F32),
                        pltpu.VMEM((CONV_HALO + tq, D_GROUP), F32),
                        pltpu.VMEM((SWA_WINDOW + tq, SWA_KV_DIM), F32),
                        pltpu.VMEM((SWA_WINDOW + tq, SWA_KV_DIM), F32)],
        compiler_params=_params("arbitrary", vmem=PROMPT_LAYER_VMEM_BYTES),
        name="prompt_layer",
    )(sinks, x, x, g_pre, w_in, mk, mv, w_pool, pool_scale, conv_w, w_out, g_post)


def _sample_mixer_kernel(sinks_ref, proj_ref, pool_ref, conv_ref, kc_ref, vc_ref, mk_ref, mv_ref,
                         wpool_ref, pscale_ref, convw_ref, *refs, l, nb, t_new, pos0, creates_states):
    mix_ref = refs[-11]
    state_refs = refs[-10:-6]
    if creates_states:
        for ref in state_refs:
            for other in range(ref.shape[0]):
                if other != l:
                    ref[other] = jnp.zeros(ref.shape[1:], F32)
        state_refs = [ref.at[l] for ref in state_refs]
    npool_ref, nconv_ref, nk_ref, nv_ref = state_refs
    uext_all, cext_all, kext_all, vext_all, qs_all, qm_all = refs[-6:]
    uext_all[...] = jnp.zeros(uext_all.shape, F32)
    cext_all[...] = jnp.zeros(cext_all.shape, F32)
    kext_all[...] = jnp.zeros(kext_all.shape, F32)
    vext_all[...] = jnp.zeros(vext_all.shape, F32)

    n_swa = SWA_Q_HEADS * t_new
    n_mem = MEM_HEADS * t_new
    low = lax.broadcasted_iota(jnp.int32, (t_new, LANES), 1) < SWA_HEAD_DIM
    rows_of = lambda a: slice(a * t_new, (a + 1) * t_new)

    deltas = []
    for b in range(nb):
        r = rows_of(b)
        uext_ref, cext_ref, kext_ref, vext_ref = uext_all.at[b], cext_all.at[b], kext_all.at[b], vext_all.at[b]
        uext_ref[POOL_HALO - POOL_PAD:POOL_HALO, :] = pool_ref[b]
        deltas.append(_pool_delta(uext_ref, proj_ref[r, OFF_U:OFF_U + D_GROUP], pos0))
        npool_ref[b] = uext_ref[t_new + POOL_HALO - POOL_PAD:t_new + POOL_HALO, :]

        cext_ref[CONV_HALO - 2:CONV_HALO, :] = conv_ref[b]
        mix_ref[r, MIX_CONV:MIX_CONV + D_GROUP] = _conv_mixer(
            cext_ref, proj_ref[r, OFF_HC:OFF_HC + D_GROUP], proj_ref[r, OFF_GB:OFF_GB + D_GROUP],
            proj_ref[r, OFF_GC:OFF_GC + D_GROUP], convw_ref)
        nconv_ref[b] = cext_ref[t_new + CONV_HALO - 2:t_new + CONV_HALO, :]

        kext_ref[0:SWA_WINDOW, :] = kc_ref[b]
        vext_ref[0:SWA_WINDOW, :] = vc_ref[b]
        kext_ref[SWA_WINDOW:SWA_WINDOW + t_new, :] = proj_ref[r, OFF_K:OFF_K + SWA_KV_DIM]
        vext_ref[SWA_WINDOW:SWA_WINDOW + t_new, :] = proj_ref[r, OFF_V:OFF_V + SWA_KV_DIM]
        nk_ref[b] = kext_ref[t_new:t_new + SWA_WINDOW, :]
        nv_ref[b] = vext_ref[t_new:t_new + SWA_WINDOW, :]
        for pair in range(SWA_Q_HEADS // 2):
            qp = proj_ref[r, OFF_Q + pair * LANES:OFF_Q + (pair + 1) * LANES] * SWA_SCALE
            qr = pltpu.roll(qp, SWA_HEAD_DIM, 1)
            if pair // 2 == 0:
                q_even, q_odd = jnp.where(low, qp, 0.0), jnp.where(low, qr, 0.0)
            else:
                q_even, q_odd = jnp.where(low, 0.0, qr), jnp.where(low, 0.0, qp)
            qs_all[b, rows_of(2 * pair), :] = q_even
            qs_all[b, rows_of(2 * pair + 1), :] = q_odd
        for hm in range(MEM_HEADS):
            qm_all[b, rows_of(hm), :] = (
                proj_ref[r, OFF_QM + hm * MEM_HEAD_DIM:OFF_QM + (hm + 1) * MEM_HEAD_DIM])

    deltas = [jnp.concatenate([d[gi] for d in deltas], axis=0) for gi in range(len(POOL_WINDOWS))]
    for gi, y in enumerate(_pool_project(deltas, wpool_ref, pscale_ref)):
        mix_ref[:, MIX_POOL + gi * POOL_CH:MIX_POOL + (gi + 1) * POOL_CH] = y

    nt_dims = (((1,), (1,)), ((), ()))
    s_swa = jnp.concatenate(
        [lax.dot_general(qs_all[b].astype(BF16), kext_all[b].astype(BF16),
                         nt_dims, preferred_element_type=F32) for b in range(nb)], axis=0)
    s_mem = jnp.concatenate(
        [lax.dot_general(qm_all[b].astype(BF16), mk_ref[b].astype(BF16),
                         nt_dims, preferred_element_type=F32) for b in range(nb)], axis=0)

    ri = lax.broadcasted_iota(jnp.int32, s_swa.shape, 0)
    tok = ri % t_new
    kj = lax.broadcasted_iota(jnp.int32, s_swa.shape, 1)
    valid = (kj > tok) & (kj <= tok + SWA_WINDOW) & (kj >= max(SWA_WINDOW - pos0, 0))
    head = (lax.broadcasted_iota(jnp.int32, (s_swa.shape[0], 1), 0) // t_new) % SWA_Q_HEADS
    sink = jnp.zeros(head.shape, F32)
    for hq in range(SWA_Q_HEADS):
        sink = jnp.where(head == hq, sinks_ref[l, hq], sink)
    s_swa = jnp.where(valid, s_swa, -jnp.inf)
    m = jnp.maximum(jnp.max(s_swa, axis=1, keepdims=True), sink)
    e = jnp.exp(s_swa - m)
    p_swa = (e * (1.0 / (jnp.sum(e, axis=1, keepdims=True) + jnp.exp(sink - m)))).astype(BF16)

    mem_valid = (lax.broadcasted_iota(jnp.int32, s_mem.shape, 1) % MEM_HEADS
                 == (lax.broadcasted_iota(jnp.int32, s_mem.shape, 0) // t_new) % MEM_HEADS)
    s_mem = jnp.where(mem_valid, s_mem * MEM_SCALE, -jnp.inf)
    e = jnp.exp(s_mem - jnp.max(s_mem, axis=1, keepdims=True))
    p_mem = (e * (1.0 / jnp.sum(e, axis=1, keepdims=True))).astype(BF16)

    for b in range(nb):
        r = rows_of(b)
        o = jnp.dot(p_swa[b * n_swa:(b + 1) * n_swa, :], vext_all[b].astype(BF16), preferred_element_type=F32)
        o_roll = pltpu.roll(o, SWA_HEAD_DIM, 1)
        for pair in range(SWA_Q_HEADS // 2):
            src_even, src_odd = (o, o_roll) if pair // 2 == 0 else (o_roll, o)
            mix_ref[r, MIX_SWA + pair * LANES:MIX_SWA + (pair + 1) * LANES] = jnp.where(
                low, src_even[rows_of(2 * pair), :], src_odd[rows_of(2 * pair + 1), :])
        o = jnp.dot(p_mem[b * n_mem:(b + 1) * n_mem, :], mv_ref[b].astype(BF16), preferred_element_type=F32)
        for hm in range(MEM_HEADS):
            mix_ref[r, MIX_MEM + hm * MEM_HEAD_DIM:MIX_MEM + (hm + 1) * MEM_HEAD_DIM] = o[rows_of(hm), :]


def _sample_mixer(sinks, proj, pool, conv, kc, vc, mk, mv, w_pool, pool_scale, conv_w, new_states,
                  l, nb, t_new, pos0):
    b = proj.shape[0] // t_new
    slots = nb
    blk = lambda a: pl.BlockSpec((None, nb) + a.shape[2:], lambda i: (l, i) + (0,) * (a.ndim - 2))
    out_sds = lambda a: jax.ShapeDtypeStruct(a.shape, F32)
    n_in = 11
    carried = [] if new_states is None else list(new_states)
    if carried:
        state_blk = blk
    else:
        state_blk = lambda a: pl.BlockSpec((a.shape[0], nb) + a.shape[2:], lambda i: (0, i) + (0,) * (a.ndim - 2))
    return pl.pallas_call(
        functools.partial(_sample_mixer_kernel, l=l, nb=nb, t_new=t_new, pos0=pos0, creates_states=not carried),
        grid=(b // nb,),
        in_specs=[pl.BlockSpec(memory_space=pltpu.SMEM),
                  pl.BlockSpec((nb * t_new, D_IN), lambda i: (i, 0)),
                  blk(pool), blk(conv), blk(kc), blk(vc), blk(mk), blk(mv),
                  _layer_block(w_pool, l), _layer_block(pool_scale, l), _layer_block(conv_w, l)]
                 + [pl.BlockSpec(memory_space=pl.ANY)] * len(carried),
        input_output_aliases={n_in + k: 1 + k for k in range(len(carried))},
        out_specs=[pl.BlockSpec((nb * t_new, 4 * D_GROUP), lambda i: (i, 0)),
                   state_blk(pool), state_blk(conv), state_blk(kc), state_blk(vc)],
        out_shape=[jax.ShapeDtypeStruct((b * t_new, 4 * D_GROUP), F32),
                   out_sds(pool), out_sds(conv), out_sds(kc), out_sds(vc)],
        scratch_shapes=[pltpu.VMEM((slots, POOL_HALO + 8, D_GROUP), F32),
                        pltpu.VMEM((slots, CONV_HALO + 8, D_GROUP), F32),
                        pltpu.VMEM((slots, 2 * SWA_WINDOW, SWA_KV_DIM), F32),
                        pltpu.VMEM((slots, 2 * SWA_WINDOW, SWA_KV_DIM), F32),
                        pltpu.VMEM((slots, SWA_Q_HEADS * t_new, LANES), F32),
                        pltpu.VMEM((slots, MEM_HEADS * t_new, MEM_HEAD_DIM), F32)],
        compiler_params=_params("arbitrary"),
        name="sample_mixer",
    )(sinks, proj, pool, conv, kc, vc, mk, mv, w_pool, pool_scale, conv_w, *carried)


def kernel(x_prompt, x_sample, mem_prompt, state_pool, state_conv, cache_swa_k, cache_swa_v,
           cache_mem_k, cache_mem_v, g_mix_pre, w_in, w_pool, pool_scale, conv_w, swa_sinks,
           g_mem, w_mem_kv, w_out, g_mix_post, g_mlp_pre, w_up, w_down, g_mlp_post):
    depth = w_in.shape[0]
    bp, seq, d_model = x_prompt.shape
    bs, t_new, _ = x_sample.shape
    nmem = mem_prompt.shape[1]
    assert w_in.shape[2] == D_IN and d_model == 4 * D_GROUP

    w_pool_b = w_pool.astype(BF16)
    rows = lambda a: a.reshape(depth, 1, a.shape[-1])
    g_mix_pre, g_mem, g_mix_post = rows(g_mix_pre), rows(g_mem), rows(g_mix_post)
    g_mlp_pre, g_mlp_post, pool_scale = rows(g_mlp_pre), rows(g_mlp_post), rows(pool_scale)

    yp = x_prompt.reshape(bp * seq, d_model)
    ys = x_sample.reshape(bs * t_new, d_model)
    mem = mem_prompt.reshape(bp * nmem, d_model)
    kc = cache_swa_k.reshape(depth, bs, SWA_WINDOW, SWA_KV_DIM)
    vc = cache_swa_v.reshape(depth, bs, SWA_WINDOW, SWA_KV_DIM)
    mkc = cache_mem_k.reshape(depth, bs, nmem * MEM_HEADS, MEM_HEAD_DIM)
    mvc = cache_mem_v.reshape(depth, bs, nmem * MEM_HEADS, MEM_HEAD_DIM)

    outs = [[] for _ in range(6)]
    sample_states = None
    for l in range(depth):
        proj, w_in_b = _norm_matmul_cast(ys, g_mix_pre, w_in, l, tn=256)
        mix, *sample_states = _sample_mixer(
            swa_sinks, proj, state_pool, state_conv, kc, vc, mkc, mvc, w_pool_b, pool_scale, conv_w,
            sample_states, l, nb=8, t_new=t_new, pos0=PAST_LEN)
        ys, w_out_b = _matmul_norm_res_cast(mix, w_out, g_mix_post, ys, l, tk=512)
        ys, w_up_b, w_down_b = _mlp(ys, g_mlp_pre, w_up, w_down, g_mlp_post, l, tm=bs * t_new, tf=512)

        mk, mv = _norm_matmul(mem, g_mem, w_mem_kv, l, tm=1024, n_out=2)
        mk, mv = mk.reshape(bp, nmem, D_GROUP), mv.reshape(bp, nmem, D_GROUP)
        yp, pool_p, conv_p, k_p, v_p = _prompt_layer(swa_sinks, yp, g_mix_pre, w_in_b, mk, mv, w_pool_b, pool_scale,
                                                     conv_w, w_out_b, g_mix_post, l, tq=256, seq=seq)
        yp = _mlp(yp, g_mlp_pre, w_up_b, w_down_b, g_mlp_post, l, tm=1024, tf=512)

        for lst, val in zip(outs, (pool_p, conv_p, k_p, v_p, mk, mv)):
            lst.append(val)

    pool_p, conv_p, k_p, v_p, mk, mv = (jnp.stack(o) for o in outs)
    pool_s, conv_s, k_s, v_s = sample_states
    kv_shape = lambda a: a.reshape(a.shape[:2] + (SWA_WINDOW, 2, SWA_HEAD_DIM))
    mem_shape = lambda a: a.reshape(depth, bp, nmem, MEM_HEADS, MEM_HEAD_DIM)
    return (yp.reshape(bp, seq, d_model), ys.reshape(bs, t_new, d_model), pool_p, pool_s, conv_p, conv_s,
            kv_shape(k_p), kv_shape(k_s), kv_shape(v_p), kv_shape(v_s), mem_shape(mk), mem_shape(mv))
```

```python
import functools
import math

import jax
import jax.numpy as jnp
from jax import lax
from jax.experimental import pallas as pl
from jax.experimental.pallas import tpu as pltpu

F32 = jnp.float32
BF16 = jnp.bfloat16

LANES = 128
D_GROUP = 512
POOL_WINDOWS = (2, 4, 8, 16)
POOL_CH = D_GROUP // len(POOL_WINDOWS)
POOL_PAD = max(POOL_WINDOWS) - 1
POOL_HALO = 16
CONV_WIDTH = 3
CONV_HALO = 8
SWA_WINDOW = 128
SWA_HEAD_DIM = 64
SWA_Q_HEADS = 8
SWA_GROUP = 4
SWA_KV_DIM = 128
SWA_SCALE = 1.0 / math.sqrt(SWA_HEAD_DIM)
MEM_HEADS = 4
MEM_HEAD_DIM = D_GROUP // MEM_HEADS
MEM_SCALE = 1.0 / math.sqrt(MEM_HEAD_DIM)
RMS_EPS = 1e-6
PAST_LEN = 8192
OFF_U, OFF_HC, OFF_GB, OFF_GC, OFF_Q = 0, 512, 1024, 1536, 2048
OFF_K, OFF_V, OFF_QM, D_IN = 2560, 2688, 2816, 3328
MIX_POOL, MIX_CONV, MIX_SWA, MIX_MEM = 0, 512, 1024, 1536

VMEM_LIMIT_BYTES = 56 * 1024 * 1024
PROMPT_LAYER_VMEM_BYTES = 60 * 1024 * 1024


def _rmsnorm(x, g):
    ms = jnp.mean(x * x, axis=-1, keepdims=True)
    return x * lax.rsqrt(ms + RMS_EPS) * g


def _params(*sem, vmem=VMEM_LIMIT_BYTES):
    return pltpu.CompilerParams(dimension_semantics=sem, vmem_limit_bytes=vmem)


def _whole_block(a):
    zeros = (0,) * a.ndim
    return pl.BlockSpec(a.shape, lambda *_: zeros, pipeline_mode=pl.Buffered(1))


def _layer_block(a, l):
    rest = (0,) * (a.ndim - 1)
    return pl.BlockSpec((None,) + a.shape[1:], lambda *_: (l,) + rest, pipeline_mode=pl.Buffered(1))


def _norm_matmul_kernel(x_ref, g_ref, w_ref, *o_refs):
    h = _rmsnorm(x_ref[...], g_ref[...]).astype(BF16)
    w = w_ref[...]
    if w.dtype != BF16:
        w = w.astype(BF16)
    y = jnp.dot(h, w, preferred_element_type=F32)
    n = y.shape[1] // len(o_refs)
    for i, o_ref in enumerate(o_refs):
        o_ref[...] = y[:, i * n:(i + 1) * n]


def _norm_matmul(x, g, w, l, tm, n_out=1):
    m, k = x.shape
    n = w.shape[2] // n_out
    outs = pl.pallas_call(
        _norm_matmul_kernel,
        grid=(m // tm,),
        in_specs=[pl.BlockSpec((tm, k), lambda i: (i, 0)), _layer_block(g, l), _layer_block(w, l)],
        out_specs=[pl.BlockSpec((tm, n), lambda i: (i, 0))] * n_out,
        out_shape=[jax.ShapeDtypeStruct((m, n), F32)] * n_out,
        compiler_params=_params("arbitrary"),
        name="norm_matmul",
    )(x, g, w)
    return outs[0] if n_out == 1 else outs


def _norm_matmul_cast_kernel(x_ref, g_ref, w_ref, o_ref, wb_ref, h_ref):
    @pl.when(pl.program_id(0) == 0)
    def _():
        h_ref[...] = _rmsnorm(x_ref[...], g_ref[...]).astype(BF16)

    w = w_ref[...].astype(BF16)
    wb_ref[...] = w
    o_ref[...] = jnp.dot(h_ref[...], w, preferred_element_type=F32)


def _norm_matmul_cast(x, g, w, l, tn):
    m, k = x.shape
    n = w.shape[2]
    return pl.pallas_call(
        _norm_matmul_cast_kernel,
        grid=(n // tn,),
        in_specs=[pl.BlockSpec((m, k), lambda j: (0, 0)), _layer_block(g, l),
                  pl.BlockSpec((None, k, tn), lambda j: (l, 0, j))],
        out_specs=[pl.BlockSpec((m, tn), lambda j: (0, j)), pl.BlockSpec((k, tn), lambda j: (0, j))],
        out_shape=[jax.ShapeDtypeStruct((m, n), F32), jax.ShapeDtypeStruct((k, n), BF16)],
        scratch_shapes=[pltpu.VMEM((m, k), BF16)],
        compiler_params=_params("arbitrary"),
        name="norm_matmul_cast",
    )(x, g, w)


def _matmul_norm_res_cast_kernel(a_ref, w_ref, g_ref, x_ref, o_ref, wb_ref):
    kk = pl.program_id(0)
    w = w_ref[...].astype(BF16)
    wb_ref[...] = w
    part = jnp.dot(a_ref[...].astype(BF16), w, preferred_element_type=F32)

    @pl.when(kk == 0)
    def _():
        o_ref[...] = part

    @pl.when(kk > 0)
    def _():
        o_ref[...] += part

    @pl.when(kk == pl.num_programs(0) - 1)
    def _():
        o_ref[...] = x_ref[...] + _rmsnorm(o_ref[...], g_ref[...])


def _matmul_norm_res_cast(a, w, g, x, l, tk):
    m, k = a.shape
    n = w.shape[2]
    return pl.pallas_call(
        _matmul_norm_res_cast_kernel,
        grid=(k // tk,),
        in_specs=[pl.BlockSpec((m, tk), lambda kk: (0, kk)), pl.BlockSpec((None, tk, n), lambda kk: (l, kk, 0)),
                  _layer_block(g, l), pl.BlockSpec((m, n), lambda kk: (0, 0))],
        out_specs=[pl.BlockSpec((m, n), lambda kk: (0, 0)), pl.BlockSpec((tk, n), lambda kk: (kk, 0))],
        out_shape=[jax.ShapeDtypeStruct((m, n), F32), jax.ShapeDtypeStruct((k, n), BF16)],
        compiler_params=_params("arbitrary"),
        name="matmul_norm_res_cast",
    )(a, w, g, x)


def _mlp_kernel(x_ref, gpre_ref, wup_ref, wdown_ref, gpost_ref, o_ref, *rest, cast_weights):
    f = pl.program_id(1)
    h_ref = rest[-1]

    def ff_chunk(first):
        if cast_weights:
            wup_b_ref, wdown_b_ref = rest[:2]
            wup, wdown = wup_ref[...].astype(BF16), wdown_ref[...].astype(BF16)
            wup_b_ref[...] = wup
            wdown_b_ref[...] = wdown
        else:
            wup, wdown = wup_ref[...], wdown_ref[...]
        a = jnp.dot(h_ref[...], wup, preferred_element_type=F32)
        a = jnp.square(jnp.maximum(a, 0.0)).astype(BF16)
        if first:
            o_ref[...] = jnp.dot(a, wdown, preferred_element_type=F32)
        else:
            o_ref[...] += jnp.dot(a, wdown, preferred_element_type=F32)

    @pl.when(f == 0)
    def _():
        h_ref[...] = _rmsnorm(x_ref[...], gpre_ref[...]).astype(BF16)
        ff_chunk(True)

    last = pl.num_programs(1) - 1

    @pl.when((f > 0) & (f < last))
    def _():
        ff_chunk(False)

    @pl.when(f == last)
    def _():
        ff_chunk(False)
        o_ref[...] = x_ref[...] + _rmsnorm(o_ref[...], gpost_ref[...])


def _mlp(x, g_pre, w_up, w_down, g_post, l, tm, tf):
    m, d = x.shape
    cast_weights = w_up.ndim == 3
    ff = w_up.shape[-1]
    if cast_weights:
        assert m == tm, "each weight block must be visited exactly once"
        w_specs = [pl.BlockSpec((None, d, tf), lambda i, f: (l, 0, f)),
                   pl.BlockSpec((None, tf, d), lambda i, f: (l, f, 0))]
    else:
        w_specs = [pl.BlockSpec((d, tf), lambda i, f: (0, f)), pl.BlockSpec((tf, d), lambda i, f: (f, 0))]
    out_specs = [pl.BlockSpec((tm, d), lambda i, f: (i, 0))]
    out_shape = [jax.ShapeDtypeStruct((m, d), F32)]
    if cast_weights:
        out_specs += [pl.BlockSpec((d, tf), lambda i, f: (0, f)), pl.BlockSpec((tf, d), lambda i, f: (f, 0))]
        out_shape += [jax.ShapeDtypeStruct((d, ff), BF16), jax.ShapeDtypeStruct((ff, d), BF16)]
    outs = pl.pallas_call(
        functools.partial(_mlp_kernel, cast_weights=cast_weights),
        grid=(m // tm, ff // tf),
        in_specs=[pl.BlockSpec((tm, d), lambda i, f: (i, 0)), _layer_block(g_pre, l)] + w_specs
                 + [_layer_block(g_post, l)],
        out_specs=out_specs,
        out_shape=out_shape,
        scratch_shapes=[pltpu.VMEM((tm, d), BF16)],
        compiler_params=_params("arbitrary", "arbitrary", vmem=PROMPT_LAYER_VMEM_BYTES),
        name="mlp",
    )(x, g_pre, w_up, w_down, g_post)
    return outs if cast_weights else outs[0]


def _pool_delta(uext_ref, u, pos0):
    rows = u.shape[0]
    uext_ref[POOL_HALO:POOL_HALO + rows, :] = u
    return [_pool_group_delta(uext_ref, rows, pos0, gi) for gi in range(len(POOL_WINDOWS))]


def _pool_group_delta(uext_ref, rows, pos0, gi):
    w = POOL_WINDOWS[gi]
    cols = slice(gi * POOL_CH, (gi + 1) * POOL_CH)
    u = uext_ref[POOL_HALO:POOL_HALO + rows, cols]
    s = u
    for back in range(1, w):
        s = s + uext_ref[POOL_HALO - back:POOL_HALO - back + rows, cols]
    pos = pos0 + lax.broadcasted_iota(jnp.int32, (rows, POOL_CH), 0)
    cnt = jnp.minimum(pos + 1, w).astype(F32)
    return s / cnt - u


def _pool_group_project(gi, d, wpool_ref, pscale_ref):
    y = jnp.dot(d.astype(BF16), wpool_ref[gi], preferred_element_type=F32)
    return y * pscale_ref[:, gi * POOL_CH:(gi + 1) * POOL_CH]


def _pool_project(deltas, wpool_ref, pscale_ref):
    return [_pool_group_project(gi, d, wpool_ref, pscale_ref) for gi, d in enumerate(deltas)]


def _conv_mixer(cext_ref, hc, gb, gc, convw_ref):
    rows = hc.shape[0]
    cext_ref[CONV_HALO:CONV_HALO + rows, :] = gc * hc
    conv = convw_ref[0:1, :] * cext_ref[CONV_HALO - 2:CONV_HALO - 2 + rows, :]
    for kk in range(1, CONV_WIDTH):
        lo = CONV_HALO - 2 + kk
        conv = conv + convw_ref[kk:kk + 1, :] * cext_ref[lo:lo + rows, :]
    return gb * conv


def _swa_kv_head(h, q_of_pair, kwin, vwin, sinks_ref, l, first_key):
    nk = kwin.shape[0]
    low = lax.broadcasted_iota(jnp.int32, (nk, LANES), 1) < SWA_HEAD_DIM
    kroll = pltpu.roll(kwin, SWA_HEAD_DIM, 1)
    vroll = pltpu.roll(vwin, SWA_HEAD_DIM, 1)
    ind_a = jnp.where(low, 1.0, 0.0)
    ind_b = 1.0 - ind_a
    if h == 0:
        k_a, k_b = jnp.where(low, kwin, 0.0), jnp.where(low, 0.0, kroll)
        v_a, v_b = jnp.where(low, vwin, 0.0), jnp.where(low, 0.0, vroll)
    else:
        k_a, k_b = jnp.where(low, kroll, 0.0), jnp.where(low, 0.0, kwin)
        v_a, v_b = jnp.where(low, vroll, 0.0), jnp.where(low, 0.0, vwin)
    kst = jnp.concatenate([k_a, k_b], axis=0).astype(BF16)
    vst = jnp.concatenate([jnp.concatenate([v_a, ind_a], axis=1),
                           jnp.concatenate([v_b, ind_b], axis=1)], axis=0).astype(BF16)
    outs = {}
    for r in range(2):
        pair = 2 * h + r
        q = q_of_pair(pair)
        rows = q.shape[0]
        qi = lax.broadcasted_iota(jnp.int32, (rows, 2 * nk), 0)
        kj = lax.broadcasted_iota(jnp.int32, (rows, 2 * nk), 1) & (nk - 1)
        valid = (kj > qi) & (kj <= qi + SWA_WINDOW) & (kj >= first_key)
        s = lax.dot_general((q * SWA_SCALE).astype(BF16), kst, (((1,), (1,)), ((), ())),
                            preferred_element_type=F32)
        s = jnp.where(valid, s, -jnp.inf)
        sink_a = sinks_ref[l, 4 * h + 2 * r]
        sink_b = sinks_ref[l, 4 * h + 2 * r + 1]
        m_a = jnp.maximum(jnp.max(s[:, :nk], axis=1, keepdims=True), sink_a)
        m_b = jnp.maximum(jnp.max(s[:, nk:], axis=1, keepdims=True), sink_b)
        p = jnp.concatenate([jnp.exp(s[:, :nk] - m_a), jnp.exp(s[:, nk:] - m_b)],
                            axis=1).astype(BF16)
        o = jnp.dot(p, vst, preferred_element_type=F32)
        low_q = lax.broadcasted_iota(jnp.int32, (rows, LANES), 1) < SWA_HEAD_DIM
        den = o[:, LANES:] + jnp.where(low_q, jnp.exp(sink_a - m_a), jnp.exp(sink_b - m_b))
        outs[pair] = o[:, :LANES] / den
    return outs


def _mem_head(qm_h, mk_h, mv_h):
    s = lax.dot_general(qm_h.astype(BF16), mk_h, (((1,), (1,)), ((), ())),
                        preferred_element_type=F32) * MEM_SCALE
    p = jnp.exp(s - jnp.max(s, axis=1, keepdims=True)).astype(BF16)
    vext = jnp.concatenate([mv_h, jnp.ones(mv_h.shape, BF16)], axis=1)
    o = jnp.dot(p, vext, preferred_element_type=F32)
    return o[:, :MEM_HEAD_DIM] / o[:, MEM_HEAD_DIM:]


PROJ_CHUNK = 512
N_PROJ_CHUNKS = -(-D_IN // PROJ_CHUNK)


def _prompt_layer_kernel(sinks_ref, xp_ref, xn_ref, gpre_ref, win_ref, mk_ref, mv_ref, wpool_ref, pscale_ref,
                         convw_ref, wout_ref, gpost_ref,
                         o_ref, npool_ref, nconv_ref, nk_ref, nv_ref,
                         proja_ref, projb_ref, h_ref, mixa_ref, mixb_ref, y_ref, uext_ref, cext_ref, kext_ref,
                         vext_ref, *, l, tq, nblk, nsteps):
    r = pl.program_id(0)
    j = r % nblk
    half = tq // 2

    def project_chunk(dst_ref, c):
        cols = slice(c * PROJ_CHUNK, min((c + 1) * PROJ_CHUNK, D_IN))
        dst_ref[:, cols] = jnp.dot(h_ref[...], win_ref[:, cols], preferred_element_type=F32)

    n_out_chunks = wout_ref.shape[1] // PROJ_CHUNK
    n_norm_parts = 4

    def out_chunk(m_ref, c):
        cols = slice(c * PROJ_CHUNK, (c + 1) * PROJ_CHUNK)
        y_ref[:, cols] = jnp.dot(m_ref[...], wout_ref[:, cols], preferred_element_type=F32)

    def finish_half(k):
        rows = slice(k * half, (k + 1) * half)
        o_ref[rows, :] = xp_ref[rows, :] + _rmsnorm(y_ref[rows, :], gpost_ref[...])

    def norm_part(k):
        rows = slice(k * tq // n_norm_parts, (k + 1) * tq // n_norm_parts)
        h_ref[rows, :] = _rmsnorm(xn_ref[rows, :], gpre_ref[...]).astype(BF16)

    @pl.when(r == 0)
    def _():
        h_ref[...] = _rmsnorm(xp_ref[...], gpre_ref[...]).astype(BF16)
        for c in range(N_PROJ_CHUNKS):
            project_chunk(proja_ref, c)
        mixb_ref[...] = jnp.zeros(mixb_ref.shape, BF16)

    @pl.when(j == 0)
    def _():
        uext_ref[0:POOL_HALO, :] = jnp.zeros((POOL_HALO, D_GROUP), F32)
        cext_ref[0:CONV_HALO, :] = jnp.zeros((CONV_HALO, D_GROUP), F32)
        kext_ref[0:SWA_WINDOW, :] = jnp.zeros((SWA_WINDOW, SWA_KV_DIM), F32)
        vext_ref[0:SWA_WINDOW, :] = jnp.zeros((SWA_WINDOW, SWA_KV_DIM), F32)

    def mix_block(p_ref, pn_ref, m_ref, mp_ref):
        proj_chunks = [functools.partial(project_chunk, pn_ref, c) for c in range(N_PROJ_CHUNKS)]

        def put(c0, y, rows=slice(None)):
            m_ref[rows, c0:c0 + y.shape[1]] = y.astype(BF16)

        def pool_fill_piece():
            uext_ref[POOL_HALO:POOL_HALO + tq, :] = p_ref[:, OFF_U:OFF_U + D_GROUP]
            npool_ref[...] = uext_ref[tq + POOL_HALO - POOL_PAD:tq + POOL_HALO, :]

        def pool_group_piece(gi):
            d = _pool_group_delta(uext_ref, tq, j * tq, gi)
            put(MIX_POOL + gi * POOL_CH, _pool_group_project(gi, d, wpool_ref, pscale_ref))

        def pool_carry_piece():
            uext_ref[0:POOL_HALO, :] = uext_ref[tq:tq + POOL_HALO, :]

        def conv_piece():
            y = _conv_mixer(cext_ref, p_ref[:, OFF_HC:OFF_HC + D_GROUP], p_ref[:, OFF_GB:OFF_GB + D_GROUP],
                            p_ref[:, OFF_GC:OFF_GC + D_GROUP], convw_ref)
            put(MIX_CONV, y)
            nconv_ref[...] = cext_ref[tq + CONV_HALO - 2:tq + CONV_HALO, :]
            cext_ref[0:CONV_HALO, :] = cext_ref[tq:tq + CONV_HALO, :]

        def kv_fill_piece():
            kext_ref[SWA_WINDOW:SWA_WINDOW + tq, :] = p_ref[:, OFF_K:OFF_K + SWA_KV_DIM]
            vext_ref[SWA_WINDOW:SWA_WINDOW + tq, :] = p_ref[:, OFF_V:OFF_V + SWA_KV_DIM]
            nk_ref[...] = kext_ref[tq:tq + SWA_WINDOW, :]
            nv_ref[...] = vext_ref[tq:tq + SWA_WINDOW, :]

        def swa_piece(sb, h):
            r0 = sb * SWA_WINDOW
            rows = slice(r0, r0 + SWA_WINDOW)
            first_key = jnp.maximum(SWA_WINDOW - (j * tq + r0), 0)
            q_of_pair = lambda pair: p_ref[rows, OFF_Q + pair * LANES:OFF_Q + (pair + 1) * LANES]
            outs = _swa_kv_head(h, q_of_pair, kext_ref[r0:r0 + 2 * SWA_WINDOW, :],
                                vext_ref[r0:r0 + 2 * SWA_WINDOW, :], sinks_ref, l, first_key)
            for pair, y in outs.items():
                put(MIX_SWA + pair * LANES, y, rows)

        def kv_carry_piece():
            kext_ref[0:SWA_WINDOW, :] = kext_ref[tq:tq + SWA_WINDOW, :]
            vext_ref[0:SWA_WINDOW, :] = vext_ref[tq:tq + SWA_WINDOW, :]

        def mem_piece(hm):
            c0 = hm * MEM_HEAD_DIM
            y = _mem_head(p_ref[:, OFF_QM + c0:OFF_QM + c0 + MEM_HEAD_DIM],
                          mk_ref[:, c0:c0 + MEM_HEAD_DIM].astype(BF16), mv_ref[:, c0:c0 + MEM_HEAD_DIM].astype(BF16))
            put(MIX_MEM + c0, y)

        mxu_items = [functools.partial(out_chunk, mp_ref, c) for c in range(n_out_chunks)] + proj_chunks
        vpu_items = [functools.partial(norm_part, k) for k in range(n_norm_parts)]
        vpu_items += [pool_fill_piece] + [functools.partial(pool_group_piece, gi) for gi in range(len(POOL_WINDOWS))]
        vpu_items += [pool_carry_piece, functools.partial(finish_half, 0), conv_piece,
                      functools.partial(finish_half, 1), kv_fill_piece]
        vpu_items += [functools.partial(swa_piece, sb, h) for sb in range(tq // SWA_WINDOW) for h in range(2)]
        vpu_items += [kv_carry_piece] + [functools.partial(mem_piece, hm) for hm in range(MEM_HEADS)]
        mxu_items[0]()
        rest = mxu_items[1:]
        for i, item in enumerate(vpu_items):
            item()
            for c in range(i * len(rest) // len(vpu_items), (i + 1) * len(rest) // len(vpu_items)):
                rest[c]()

    @pl.when((r < nsteps) & (r % 2 == 0))
    def _():
        mix_block(proja_ref, projb_ref, mixa_ref, mixb_ref)

    @pl.when((r < nsteps) & (r % 2 == 1))
    def _():
        mix_block(projb_ref, proja_ref, mixb_ref, mixa_ref)

    @pl.when(r == nsteps)
    def _():
        for c in range(n_out_chunks):
            out_chunk(mixb_ref if nsteps % 2 == 0 else mixa_ref, c)
        finish_half(0)
        finish_half(1)


def _prompt_layer(sinks, x, g_pre, w_in, mk, mv, w_pool, pool_scale, conv_w, w_out, g_post, l, tq, seq):
    m, d = x.shape
    nblk, nsteps, b = seq // tq, m // tq, m // seq
    nmem = mk.shape[1]
    assert nsteps >= 2
    per_seq = lambda r: (jnp.minimum(r // nblk, b - 1), 0, 0)
    prev_block = lambda r: (jnp.maximum(r - 1, 0), 0)
    return pl.pallas_call(
        functools.partial(_prompt_layer_kernel, l=l, tq=tq, nblk=nblk, nsteps=nsteps),
        grid=(nsteps + 1,),
        in_specs=[pl.BlockSpec(memory_space=pltpu.SMEM),
                  pl.BlockSpec((tq, d), prev_block),
                  pl.BlockSpec((tq, d), lambda r: (jnp.minimum(r + 1, nsteps - 1), 0)),
                  _layer_block(g_pre, l), _whole_block(w_in),
                  pl.BlockSpec((None, nmem, D_GROUP), per_seq),
                  pl.BlockSpec((None, nmem, D_GROUP), per_seq),
                  _layer_block(w_pool, l), _layer_block(pool_scale, l), _layer_block(conv_w, l),
                  _whole_block(w_out), _layer_block(g_post, l)],
        out_specs=[pl.BlockSpec((tq, d), prev_block),
                   pl.BlockSpec((None, POOL_PAD, D_GROUP), per_seq),
                   pl.BlockSpec((None, CONV_WIDTH - 1, D_GROUP), per_seq),
                   pl.BlockSpec((None, SWA_WINDOW, SWA_KV_DIM), per_seq),
                   pl.BlockSpec((None, SWA_WINDOW, SWA_KV_DIM), per_seq)],
        out_shape=[jax.ShapeDtypeStruct((m, d), F32),
                   jax.ShapeDtypeStruct((b, POOL_PAD, D_GROUP), F32),
                   jax.ShapeDtypeStruct((b, CONV_WIDTH - 1, D_GROUP), F32),
                   jax.ShapeDtypeStruct((b, SWA_WINDOW, SWA_KV_DIM), F32),
                   jax.ShapeDtypeStruct((b, SWA_WINDOW, SWA_KV_DIM), F32)],
        scratch_shapes=[pltpu.VMEM((tq, D_IN), F32),
                        pltpu.VMEM((tq, D_IN), F32),
                        pltpu.VMEM((tq, d), BF16),
                        pltpu.VMEM((tq, d), BF16),
                        pltpu.VMEM((tq, d), BF16),
                        pltpu.VMEM((tq, d), F32),
                        pltpu.VMEM((POOL_HALO + tq, D_GROUP), F32),
                        pltpu.VMEM((CONV_HALO + tq, D_GROUP), F32),
                        pltpu.VMEM((SWA_WINDOW + tq, SWA_KV_DIM), F32),
                        pltpu.VMEM((SWA_WINDOW + tq, SWA_KV_DIM), F32)],
        compiler_params=_params("arbitrary", vmem=PROMPT_LAYER_VMEM_BYTES),
        name="prompt_layer",
    )(sinks, x, x, g_pre, w_in, mk, mv, w_pool, pool_scale, conv_w, w_out, g_post)


def _sample_mixer_kernel(sinks_ref, proj_ref, pool_ref, conv_ref, kc_ref, vc_ref, mk_ref, mv_ref,
                         wpool_ref, pscale_ref, convw_ref, *refs, l, nb, t_new, pos0, creates_states):
    mix_ref = refs[-11]
    state_refs = refs[-10:-6]
    if creates_states:
        for ref in state_refs:
            for other in range(ref.shape[0]):
                if other != l:
                    ref[other] = jnp.zeros(ref.shape[1:], F32)
        state_refs = [ref.at[l] for ref in state_refs]
    npool_ref, nconv_ref, nk_ref, nv_ref = state_refs
    uext_all, cext_all, kext_all, vext_all, qs_all, qm_all = refs[-6:]
    uext_all[...] = jnp.zeros(uext_all.shape, F32)
    cext_all[...] = jnp.zeros(cext_all.shape, F32)
    kext_all[...] = jnp.zeros(kext_all.shape, F32)
    vext_all[...] = jnp.zeros(vext_all.shape, F32)

    n_swa = SWA_Q_HEADS * t_new
    n_mem = MEM_HEADS * t_new
    low = lax.broadcasted_iota(jnp.int32, (t_new, LANES), 1) < SWA_HEAD_DIM
    rows_of = lambda a: slice(a * t_new, (a + 1) * t_new)

    deltas = []
    for b in range(nb):
        r = rows_of(b)
        uext_ref, cext_ref, kext_ref, vext_ref = uext_all.at[b], cext_all.at[b], kext_all.at[b], vext_all.at[b]
        uext_ref[POOL_HALO - POOL_PAD:POOL_HALO, :] = pool_ref[b]
        deltas.append(_pool_delta(uext_ref, proj_ref[r, OFF_U:OFF_U + D_GROUP], pos0))
        npool_ref[b] = uext_ref[t_new + POOL_HALO - POOL_PAD:t_new + POOL_HALO, :]

        cext_ref[CONV_HALO - 2:CONV_HALO, :] = conv_ref[b]
        mix_ref[r, MIX_CONV:MIX_CONV + D_GROUP] = _conv_mixer(
            cext_ref, proj_ref[r, OFF_HC:OFF_HC + D_GROUP], proj_ref[r, OFF_GB:OFF_GB + D_GROUP],
            proj_ref[r, OFF_GC:OFF_GC + D_GROUP], convw_ref)
        nconv_ref[b] = cext_ref[t_new + CONV_HALO - 2:t_new + CONV_HALO, :]

        kext_ref[0:SWA_WINDOW, :] = kc_ref[b]
        vext_ref[0:SWA_WINDOW, :] = vc_ref[b]
        kext_ref[SWA_WINDOW:SWA_WINDOW + t_new, :] = proj_ref[r, OFF_K:OFF_K + SWA_KV_DIM]
        vext_ref[SWA_WINDOW:SWA_WINDOW + t_new, :] = proj_ref[r, OFF_V:OFF_V + SWA_KV_DIM]
        nk_ref[b] = kext_ref[t_new:t_new + SWA_WINDOW, :]
        nv_ref[b] = vext_ref[t_new:t_new + SWA_WINDOW, :]
        for pair in range(SWA_Q_HEADS // 2):
            qp = proj_ref[r, OFF_Q + pair * LANES:OFF_Q + (pair + 1) * LANES] * SWA_SCALE
            qr = pltpu.roll(qp, SWA_HEAD_DIM, 1)
            if pair // 2 == 0:
                q_even, q_odd = jnp.where(low, qp, 0.0), jnp.where(low, qr, 0.0)
            else:
                q_even, q_odd = jnp.where(low, 0.0, qr), jnp.where(low, 0.0, qp)
            qs_all[b, rows_of(2 * pair), :] = q_even
            qs_all[b, rows_of(2 * pair + 1), :] = q_odd
        for hm in range(MEM_HEADS):
            qm_all[b, rows_of(hm), :] = (
                proj_ref[r, OFF_QM + hm * MEM_HEAD_DIM:OFF_QM + (hm + 1) * MEM_HEAD_DIM])

    deltas = [jnp.concatenate([d[gi] for d in deltas], axis=0) for gi in range(len(POOL_WINDOWS))]
    for gi, y in enumerate(_pool_project(deltas, wpool_ref, pscale_ref)):
        mix_ref[:, MIX_POOL + gi * POOL_CH:MIX_POOL + (gi + 1) * POOL_CH] = y

    nt_dims = (((1,), (1,)), ((), ()))
    s_swa = jnp.concatenate(
        [lax.dot_general(qs_all[b].astype(BF16), kext_all[b].astype(BF16),
                         nt_dims, preferred_element_type=F32) for b in range(nb)], axis=0)
    s_mem = jnp.concatenate(
        [lax.dot_general(qm_all[b].astype(BF16), mk_ref[b].astype(BF16),
                         nt_dims, preferred_element_type=F32) for b in range(nb)], axis=0)

    ri = lax.broadcasted_iota(jnp.int32, s_swa.shape, 0)
    tok = ri % t_new
    kj = lax.broadcasted_iota(jnp.int32, s_swa.shape, 1)
    valid = (kj > tok) & (kj <= tok + SWA_WINDOW) & (kj >= max(SWA_WINDOW - pos0, 0))
    head = (lax.broadcasted_iota(jnp.int32, (s_swa.shape[0], 1), 0) // t_new) % SWA_Q_HEADS
    sink = jnp.zeros(head.shape, F32)
    for hq in range(SWA_Q_HEADS):
        sink = jnp.where(head == hq, sinks_ref[l, hq], sink)
    s_swa = jnp.where(valid, s_swa, -jnp.inf)
    m = jnp.maximum(jnp.max(s_swa, axis=1, keepdims=True), sink)
    e = jnp.exp(s_swa - m)
    p_swa = (e * (1.0 / (jnp.sum(e, axis=1, keepdims=True) + jnp.exp(sink - m)))).astype(BF16)

    mem_valid = (lax.broadcasted_iota(jnp.int32, s_mem.shape, 1) % MEM_HEADS
                 == (lax.broadcasted_iota(jnp.int32, s_mem.shape, 0) // t_new) % MEM_HEADS)
    s_mem = jnp.where(mem_valid, s_mem * MEM_SCALE, -jnp.inf)
    e = jnp.exp(s_mem - jnp.max(s_mem, axis=1, keepdims=True))
    p_mem = (e * (1.0 / jnp.sum(e, axis=1, keepdims=True))).astype(BF16)

    for b in range(nb):
        r = rows_of(b)
        o = jnp.dot(p_swa[b * n_swa:(b + 1) * n_swa, :], vext_all[b].astype(BF16), preferred_element_type=F32)
        o_roll = pltpu.roll(o, SWA_HEAD_DIM, 1)
        for pair in range(SWA_Q_HEADS // 2):
            src_even, src_odd = (o, o_roll) if pair // 2 == 0 else (o_roll, o)
            mix_ref[r, MIX_SWA + pair * LANES:MIX_SWA + (pair + 1) * LANES] = jnp.where(
                low, src_even[rows_of(2 * pair), :], src_odd[rows_of(2 * pair + 1), :])
        o = jnp.dot(p_mem[b * n_mem:(b + 1) * n_mem, :], mv_ref[b].astype(BF16), preferred_element_type=F32)
        for hm in range(MEM_HEADS):
            mix_ref[r, MIX_MEM + hm * MEM_HEAD_DIM:MIX_MEM + (hm + 1) * MEM_HEAD_DIM] = o[rows_of(hm), :]


def _sample_mixer(sinks, proj, pool, conv, kc, vc, mk, mv, w_pool, pool_scale, conv_w, new_states,
                  l, nb, t_new, pos0):
    b = proj.shape[0] // t_new
    slots = nb
    blk = lambda a: pl.BlockSpec((None, nb) + a.shape[2:], lambda i: (l, i) + (0,) * (a.ndim - 2))
    out_sds = lambda a: jax.ShapeDtypeStruct(a.shape, F32)
    n_in = 11
    carried = [] if new_states is None else list(new_states)
    if carried:
        state_blk = blk
    else:
        state_blk = lambda a: pl.BlockSpec((a.shape[0], nb) + a.shape[2:], lambda i: (0, i) + (0,) * (a.ndim - 2))
    return pl.pallas_call(
        functools.partial(_sample_mixer_kernel, l=l, nb=nb, t_new=t_new, pos0=pos0, creates_states=not carried),
        grid=(b // nb,),
        in_specs=[pl.BlockSpec(memory_space=pltpu.SMEM),
                  pl.BlockSpec((nb * t_new, D_IN), lambda i: (i, 0)),
                  blk(pool), blk(conv), blk(kc), blk(vc), blk(mk), blk(mv),
                  _layer_block(w_pool, l), _layer_block(pool_scale, l), _layer_block(conv_w, l)]
                 + [pl.BlockSpec(memory_space=pl.ANY)] * len(carried),
        input_output_aliases={n_in + k: 1 + k for k in range(len(carried))},
        out_specs=[pl.BlockSpec((nb * t_new, 4 * D_GROUP), lambda i: (i, 0)),
                   state_blk(pool), state_blk(conv), state_blk(kc), state_blk(vc)],
        out_shape=[jax.ShapeDtypeStruct((b * t_new, 4 * D_GROUP), F32),
                   out_sds(pool), out_sds(conv), out_sds(kc), out_sds(vc)],
        scratch_shapes=[pltpu.VMEM((slots, POOL_HALO + 8, D_GROUP), F32),
                        pltpu.VMEM((slots, CONV_HALO + 8, D_GROUP), F32),
                        pltpu.VMEM((slots, 2 * SWA_WINDOW, SWA_KV_DIM), F32),
                        pltpu.VMEM((slots, 2 * SWA_WINDOW, SWA_KV_DIM), F32),
                        pltpu.VMEM((slots, SWA_Q_HEADS * t_new, LANES), F32),
                        pltpu.VMEM((slots, MEM_HEADS * t_new, MEM_HEAD_DIM), F32)],
        compiler_params=_params("arbitrary"),
        name="sample_mixer",
    )(sinks, proj, pool, conv, kc, vc, mk, mv, w_pool, pool_scale, conv_w, *carried)


def kernel(x_prompt, x_sample, mem_prompt, state_pool, state_conv, cache_swa_k, cache_swa_v,
           cache_mem_k, cache_mem_v, g_mix_pre, w_in, w_pool, pool_scale, conv_w, swa_sinks,
           g_mem, w_mem_kv, w_out, g_mix_post, g_mlp_pre, w_up, w_down, g_mlp_post):
    depth = w_in.shape[0]
    bp, seq, d_model = x_prompt.shape
    bs, t_new, _ = x_sample.shape
    nmem = mem_prompt.shape[1]
    assert w_in.shape[2] == D_IN and d_model == 4 * D_GROUP

    w_pool_b = w_pool.astype(BF16)
    rows = lambda a: a.reshape(depth, 1, a.shape[-1])
    g_mix_pre, g_mem, g_mix_post = rows(g_mix_pre), rows(g_mem), rows(g_mix_post)
    g_mlp_pre, g_mlp_post, pool_scale = rows(g_mlp_pre), rows(g_mlp_post), rows(pool_scale)

    yp = x_prompt.reshape(bp * seq, d_model)
    ys = x_sample.reshape(bs * t_new, d_model)
    mem = mem_prompt.reshape(bp * nmem, d_model)
    kc = cache_swa_k.reshape(depth, bs, SWA_WINDOW, SWA_KV_DIM)
    vc = cache_swa_v.reshape(depth, bs, SWA_WINDOW, SWA_KV_DIM)
    mkc = cache_mem_k.reshape(depth, bs, nmem * MEM_HEADS, MEM_HEAD_DIM)
    mvc = cache_mem_v.reshape(depth, bs, nmem * MEM_HEADS, MEM_HEAD_DIM)

    outs = [[] for _ in range(6)]
    sample_states = None
    for l in range(depth):
        proj, w_in_b = _norm_matmul_cast(ys, g_mix_pre, w_in, l, tn=256)
        mix, *sample_states = _sample_mixer(
            swa_sinks, proj, state_pool, state_conv, kc, vc, mkc, mvc, w_pool_b, pool_scale, conv_w,
            sample_states, l, nb=8, t_new=t_new, pos0=PAST_LEN)
        ys, w_out_b = _matmul_norm_res_cast(mix, w_out, g_mix_post, ys, l, tk=512)
        ys, w_up_b, w_down_b = _mlp(ys, g_mlp_pre, w_up, w_down, g_mlp_post, l, tm=bs * t_new, tf=512)

        mk, mv = _norm_matmul(mem, g_mem, w_mem_kv, l, tm=1024, n_out=2)
        mk, mv = mk.reshape(bp, nmem, D_GROUP), mv.reshape(bp, nmem, D_GROUP)
        yp, pool_p, conv_p, k_p, v_p = _prompt_layer(swa_sinks, yp, g_mix_pre, w_in_b, mk, mv, w_pool_b, pool_scale,
                                                     conv_w, w_out_b, g_mix_post, l, tq=256, seq=seq)
        yp = _mlp(yp, g_mlp_pre, w_up_b, w_down_b, g_mlp_post, l, tm=1024, tf=1024)

        for lst, val in zip(outs, (pool_p, conv_p, k_p, v_p, mk, mv)):
            lst.append(val)

    pool_p, conv_p, k_p, v_p, mk, mv = (jnp.stack(o) for o in outs)
    pool_s, conv_s, k_s, v_s = sample_states
    kv_shape = lambda a: a.reshape(a.shape[:2] + (SWA_WINDOW, 2, SWA_HEAD_DIM))
    mem_shape = lambda a: a.reshape(depth, bp, nmem, MEM_HEADS, MEM_HEAD_DIM)
    return (yp.reshape(bp, seq, d_model), ys.reshape(bs, t_new, d_model), pool_p, pool_s, conv_p, conv_s,
            kv_shape(k_p), kv_shape(k_s), kv_shape(v_p), kv_shape(v_s), mem_shape(mk), mem_shape(mv))
```
